```python
import math
import jax, jax.numpy as jnp
from jax import lax
import numpy as np

D_MODEL = 2048
BATCH = 4
SEQ = 2048
DEPTH = 4
DEC_BATCH = 128
DEC_SEQ = 4
PAST_LEN = 16384
PAGE_SIZE = 128

N_MIXERS = 4
EXPAND = 2
D_INNER = EXPAND * D_MODEL
NORM_EPS = 1e-6
CHUNK = 64
GLA_CHUNK = 16

RWKV_HEAD = 64
RWKV_HEADS = D_INNER // RWKV_HEAD
RWKV_LORA_W = 96
RWKV_LORA_A = 96
RWKV_IN = 4 * D_INNER + RWKV_LORA_W + RWKV_LORA_A
RWKV_GN_EPS = 64e-5

RET_HEADS = 8
RET_DK = D_MODEL // RET_HEADS
RET_DV = D_INNER // RET_HEADS
RET_IN = 2 * D_MODEL + 2 * D_INNER
ROPE_BASE = 10000.0

SSD_HEADDIM = 64
SSD_HEADS = D_INNER // SSD_HEADDIM
SSD_STATE = 128
SSD_GROUPS = 8
SSD_CONV = 4
SSD_CONV_DIM = D_INNER + 2 * SSD_GROUPS * SSD_STATE
SSD_IN = D_INNER + SSD_CONV_DIM + SSD_HEADS
SSD_NORM_EPS = 1e-5

GLA_HEADS = 4
GLA_KEY = D_MODEL // 2
GLA_DK = GLA_KEY // GLA_HEADS
GLA_DV = D_INNER // GLA_HEADS
GLA_LORA = 16
GLA_LOGIT_NORM = 16.0
GLA_IN = 2 * GLA_KEY + 2 * D_INNER + GLA_LORA

kernel_name = 'hybrid_rwkv7_retnet_mamba2_gla_step'


def rmsnorm(x, g):
    x32 = x.astype(jnp.float32)
    y = x32 * lax.rsqrt(jnp.mean(x32 * x32, axis=-1, keepdims=True) + NORM_EPS)
    return (y * g.astype(jnp.float32)).astype(x.dtype)


def head_rms(y, eps):
    y32 = y.astype(jnp.float32)
    return y32 * lax.rsqrt(jnp.mean(y32 * y32, axis=-1, keepdims=True) + eps)


def chunk_len(T, c):
    return c if T % c == 0 else math.gcd(T, c)


def scalar_decay_chunked(q, k, v, log_a, S0, chunk):
    Bsz, T, G, K = q.shape
    R, V = v.shape[3], v.shape[4]
    c = chunk_len(T, chunk)
    n = T // c
    dt = v.dtype
    qc = q.reshape(Bsz, n, c, G, K)
    kc = k.reshape(Bsz, n, c, G, K)
    vc = v.reshape(Bsz, n, c, G, R, V)
    L = jnp.cumsum(log_a.astype(jnp.float32).reshape(Bsz, n, c, G, R), axis=2)
    causal = jnp.tril(jnp.ones((c, c), dtype=bool))[None, None, :, :, None, None]
    seg = L[:, :, :, None] - L[:, :, None, :]
    decay = jnp.exp(jnp.where(causal, seg, -jnp.inf)).astype(dt)
    scores = jnp.einsum('bntgk,bnsgk->bntsg', qc, kc)
    y_intra = jnp.einsum('bntsg,bntsgr,bnsgrv->bntgrv', scores, decay, vc)

    def step(S, inp):
        q_n, k_n, v_n, L_n = inp
        y_n = jnp.einsum('btgk,btgr,bgrkv->btgrv', q_n, jnp.exp(L_n).astype(dt), S)
        w_end = jnp.exp(L_n[:, -1:] - L_n).astype(dt)
        S = (jnp.exp(L_n[:, -1]).astype(dt)[..., None, None] * S
             + jnp.einsum('bsgk,bsgr,bsgrv->bgrkv', k_n, w_end, v_n))
        return S, y_n

    xs = (jnp.moveaxis(qc, 1, 0), jnp.moveaxis(kc, 1, 0), jnp.moveaxis(vc, 1, 0), jnp.moveaxis(L, 1, 0))
    S_T, y_inter = lax.scan(step, S0, xs)
    y = y_intra + jnp.moveaxis(y_inter, 0, 1)
    return y.reshape(Bsz, T, G, R, V), S_T


def vector_decay_chunked(q, k, v, log_a, S0, chunk):
    Bsz, T, H, K = q.shape
    V = v.shape[-1]
    c = chunk_len(T, chunk)
    n = T // c
    dt = v.dtype
    L = jnp.cumsum(log_a.astype(jnp.float32).reshape(Bsz, n, c, H, K), axis=2)
    qf = q.astype(jnp.float32).reshape(Bsz, n, c, H, K)
    kf = k.astype(jnp.float32).reshape(Bsz, n, c, H, K)
    vc = v.reshape(Bsz, n, c, H, V)
    qe = qf * jnp.exp(L)
    ke = kf * jnp.exp(-L)
    causal = jnp.tril(jnp.ones((c, c), dtype=bool))
    A = jnp.where(causal, jnp.einsum('bnthk,bnshk->bnhts', qe, ke), 0.0).astype(dt)
    y_intra = jnp.einsum('bnhts,bnshv->bnthv', A, vc)
    kend = (kf * jnp.exp(L[:, :, -1:] - L)).astype(dt)
    gend = jnp.exp(L[:, :, -1]).astype(dt)

    def step(S, inp):
        qe_n, kend_n, g_n, v_n = inp
        y_n = jnp.einsum('bthk,bhkv->bthv', qe_n, S)
        S = g_n[..., None] * S + jnp.einsum('bshk,bshv->bhkv', kend_n, v_n)
        return S, y_n

    xs = (jnp.moveaxis(qe.astype(dt), 1, 0), jnp.moveaxis(kend, 1, 0), jnp.moveaxis(gend, 1, 0), jnp.moveaxis(vc, 1, 0))
    S_T, y_inter = lax.scan(step, S0, xs)
    y = y_intra + jnp.moveaxis(y_inter, 0, 1)
    return y.reshape(Bsz, T, H, V), S_T


def rwkv7_scan(r, w, k, v, kk, a, S0):
    def step(S, inp):
        r_t, w_t, k_t, v_t, kk_t, a_t = inp
        sa = jnp.einsum('bhvk,bhk->bhv', S, -kk_t)
        S = (S * w_t[:, :, None, :] + sa[..., None] * (kk_t * a_t)[:, :, None, :]
             + v_t[..., None] * k_t[:, :, None, :])
        y = jnp.einsum('bhvk,bhk->bhv', S, r_t)
        return S, y

    xs = (jnp.moveaxis(r, 1, 0), jnp.moveaxis(w, 1, 0), jnp.moveaxis(k, 1, 0),
          jnp.moveaxis(v, 1, 0), jnp.moveaxis(kk, 1, 0), jnp.moveaxis(a, 1, 0))
    S_T, y = lax.scan(step, S0, xs)
    return jnp.moveaxis(y, 0, 1), S_T


def rotary(x, pos):
    half = x.shape[-1] // 2
    inv = 1.0 / (ROPE_BASE ** jnp.linspace(0.0, 1.0, half, dtype=jnp.float32))
    ang = pos.astype(jnp.float32)[:, None] * inv[None, :]
    cos = jnp.cos(ang)[None, :, None, :]
    sin = jnp.sin(ang)[None, :, None, :]
    x32 = x.astype(jnp.float32)
    x1, x2 = x32[..., :half], x32[..., half:]
    return jnp.concatenate([x1 * cos - x2 * sin, x1 * sin + x2 * cos], axis=-1).astype(x.dtype)


def rwkv7_mixer(xn, shift, S0, w_in, mu, w0, w_up, a0, a_up, k_k, k_a, r_k, ln_w, ln_b, w_out):
    Bsz, T, _ = xn.shape
    dt = xn.dtype
    p_all = jnp.concatenate([shift[:, None, :], xn], axis=1) @ w_in
    p, p_prev = p_all[:, 1:], p_all[:, :-1]
    p = p + mu * (p_prev - p)
    r, k, v, g, wd, ad = jnp.split(p, [D_INNER, 2 * D_INNER, 3 * D_INNER, 4 * D_INNER,
                                       4 * D_INNER + RWKV_LORA_W], axis=-1)
    w_raw = (w0 + jnp.tanh(wd) @ w_up).astype(jnp.float32)
    decay = jnp.exp(-jnp.exp(-jax.nn.softplus(-w_raw) - 0.5)).astype(dt)
    a = jax.nn.sigmoid(a0 + ad @ a_up)
    heads = lambda t: t.reshape(Bsz, T, RWKV_HEADS, RWKV_HEAD)
    kk32 = heads(k * k_k).astype(jnp.float32)
    kk = (kk32 / jnp.maximum(jnp.sqrt(jnp.sum(kk32 * kk32, axis=-1, keepdims=True)), 1e-12)).astype(dt)
    k = k * (1.0 + (a - 1.0) * k_a)
    r, k, v, a, decay = heads(r), heads(k), heads(v), heads(a), heads(decay)
    y, S = rwkv7_scan(r, decay, k, v, kk, a, S0)
    y32 = y.astype(jnp.float32)
    mean = jnp.mean(y32, axis=-1, keepdims=True)
    var = jnp.mean(jnp.square(y32 - mean), axis=-1, keepdims=True)
    y = ((y32 - mean) * lax.rsqrt(var + RWKV_GN_EPS)).reshape(Bsz, T, D_INNER) * ln_w + ln_b
    bonus = (jnp.sum(r * k * r_k, axis=-1, keepdims=True) * v).reshape(Bsz, T, D_INNER)
    out = ((y.astype(dt) + bonus) * jax.nn.silu(g)) @ w_out
    return out, xn[:, -1], S


def retention_mixer(xn, pos, S0, w_in, w_out):
    Bsz, T, _ = xn.shape
    q, k, v, g = jnp.split(xn @ w_in, [D_MODEL, 2 * D_MODEL, 2 * D_MODEL + D_INNER], axis=-1)
    q = rotary(q.reshape(Bsz, T, RET_HEADS, RET_DK), pos)
    k = rotary(k.reshape(Bsz, T, RET_HEADS, RET_DK), pos) * (RET_DK ** -0.5)
    v = v.reshape(Bsz, T, RET_HEADS, 1, RET_DV)
    log_gamma = jnp.log1p(-(2.0 ** (-5.0 - jnp.arange(RET_HEADS, dtype=jnp.float32))))
    log_a = jnp.broadcast_to(log_gamma[None, None, :, None], (Bsz, T, RET_HEADS, 1))
    y, S = scalar_decay_chunked(q, k, v, log_a, S0[:, :, None], CHUNK)
    y = head_rms(y.reshape(Bsz, T, RET_HEADS, RET_DV), NORM_EPS).reshape(Bsz, T, D_INNER).astype(xn.dtype)
    out = (y * jax.nn.silu(g)) @ w_out
    return out, S[:, :, 0]


def ssd_mixer(xn, conv_state, S0, w_in, conv_w, conv_b, dt_bias, A_log, D_skip, norm_w, w_out):
    Bsz, T, _ = xn.shape
    dtp = xn.dtype
    G, R = SSD_GROUPS, SSD_HEADS // SSD_GROUPS
    z, xBC, dt_raw = jnp.split(xn @ w_in, [D_INNER, D_INNER + SSD_CONV_DIM], axis=-1)
    xpad = jnp.concatenate([conv_state, xBC], axis=1)
    conv = lax.conv_general_dilated(xpad, conv_w[:, None, :], window_strides=(1,), padding='VALID',
                                    dimension_numbers=('NWC', 'WIO', 'NWC'),
                                    feature_group_count=SSD_CONV_DIM) + conv_b
    xBC_c = jax.nn.silu(conv)
    xs, Bm, Cm = jnp.split(xBC_c, [D_INNER, D_INNER + SSD_GROUPS * SSD_STATE], axis=-1)
    dt = jax.nn.softplus((dt_raw + dt_bias).astype(jnp.float32)).reshape(Bsz, T, G, R)
    A = -jnp.exp(A_log.astype(jnp.float32)).reshape(G, R)
    x_h = xs.reshape(Bsz, T, G, R, SSD_HEADDIM)
    v = x_h * dt[..., None].astype(dtp)
    y, S = scalar_decay_chunked(Cm.reshape(Bsz, T, G, SSD_STATE), Bm.reshape(Bsz, T, G, SSD_STATE), v,
                                dt * A, S0.reshape(Bsz, G, R, SSD_STATE, SSD_HEADDIM), CHUNK)
    y = y + x_h * D_skip.reshape(G, R, 1)
    y = y.reshape(Bsz, T, D_INNER) * jax.nn.silu(z)
    y = head_rms(y.reshape(Bsz, T, G, D_INNER // G), SSD_NORM_EPS).reshape(Bsz, T, D_INNER)
    out = (y * norm_w.astype(jnp.float32)).astype(dtp) @ w_out
    return out, xpad[:, -(SSD_CONV - 1):], S.reshape(Bsz, SSD_HEADS, SSD_STATE, SSD_HEADDIM)


def gla_mixer(xn, S0, w_in, a_up, a_bias, norm_w, w_out):
    Bsz, T, _ = xn.shape
    q, k, v, g, ad = jnp.split(xn @ w_in, [GLA_KEY, 2 * GLA_KEY, 2 * GLA_KEY + D_INNER,
                                           2 * GLA_KEY + 2 * D_INNER], axis=-1)
    log_a = jax.nn.log_sigmoid((ad @ a_up + a_bias).astype(jnp.float32)) / GLA_LOGIT_NORM
    q = q.reshape(Bsz, T, GLA_HEADS, GLA_DK) * (GLA_DK ** -0.5)
    k = k.reshape(Bsz, T, GLA_HEADS, GLA_DK)
    v = v.reshape(Bsz, T, GLA_HEADS, GLA_DV)
    y, S = vector_decay_chunked(q, k, v, log_a.reshape(Bsz, T, GLA_HEADS, GLA_DK), S0, GLA_CHUNK)
    y = (head_rms(y, NORM_EPS) * norm_w.astype(jnp.float32)).reshape(Bsz, T, D_INNER).astype(xn.dtype)
    out = (y * jax.nn.silu(g)) @ w_out
    return out, S


def trunk(x, pos, states, norm_gains, final_norm, rwkv_p, ret_p, ssd_p, gla_p):
    shift, wkv, ret_s, conv_s, ssd_s, gla_s = states
    for i in range(DEPTH):
        xn = rmsnorm(x, norm_gains[i])
        m = i % N_MIXERS
        if m == 0:
            out, shift, wkv = rwkv7_mixer(xn, shift, wkv, *rwkv_p)
        elif m == 1:
            out, ret_s = retention_mixer(xn, pos, ret_s, *ret_p)
        elif m == 2:
            out, conv_s, ssd_s = ssd_mixer(xn, conv_s, ssd_s, *ssd_p)
        else:
            out, gla_s = gla_mixer(xn, gla_s, *gla_p)
        x = x + out
    return rmsnorm(x, final_norm), (shift, wkv, ret_s, conv_s, ssd_s, gla_s)


def setup_inputs(seed: int = 0) -> dict:
    key = jax.random.key(seed)
    ks = iter(jax.random.split(key, 48))
    nrm = lambda shape, scale: jax.random.normal(next(ks), shape, jnp.float32) * scale
    unif = lambda shape, lo, hi: jax.random.uniform(next(ks), shape, jnp.float32, lo, hi)
    E = D_INNER
    dt0 = jnp.exp(unif((SSD_HEADS,), math.log(1e-3), math.log(1e-1)))
    inp = {}
    inp['x_prompt'] = nrm((BATCH, SEQ, D_MODEL), 1.0)
    inp['x_sample'] = nrm((DEC_BATCH, DEC_SEQ, D_MODEL), 1.0)
    inp['state_rwkv_shift'] = nrm((DEC_BATCH, D_MODEL), 1.0)
    inp['state_rwkv_wkv'] = nrm((DEC_BATCH, RWKV_HEADS, RWKV_HEAD, RWKV_HEAD), 0.3)
    inp['state_ret'] = nrm((DEC_BATCH, RET_HEADS, RET_DK, RET_DV), 0.3)
    inp['state_ssd_conv'] = nrm((DEC_BATCH, SSD_CONV - 1, SSD_CONV_DIM), 1.0)
    inp['state_ssd'] = nrm((DEC_BATCH, SSD_HEADS, SSD_STATE, SSD_HEADDIM), 0.3)
    inp['state_gla'] = nrm((DEC_BATCH, GLA_HEADS, GLA_DK, GLA_DV), 0.3)
    inp['norm_gains'] = 1.0 + nrm((DEPTH, D_MODEL), 0.02)
    inp['final_norm'] = 1.0 + nrm((D_MODEL,), 0.02)
    inp['rwkv_w_in'] = nrm((D_MODEL, RWKV_IN), D_MODEL ** -0.5)
    inp['rwkv_mu'] = unif((RWKV_IN,), 0.0, 1.0)
    inp['rwkv_w0'] = unif((E,), -6.0, 1.0)
    inp['rwkv_w_up'] = nrm((RWKV_LORA_W, E), 0.1 * RWKV_LORA_W ** -0.5)
    inp['rwkv_a0'] = nrm((E,), 0.1)
    inp['rwkv_a_up'] = nrm((RWKV_LORA_A, E), 0.1 * RWKV_LORA_A ** -0.5)
    inp['rwkv_k_k'] = 0.85 + nrm((E,), 0.02)
    inp['rwkv_k_a'] = 1.0 + nrm((E,), 0.02)
    inp['rwkv_r_k'] = nrm((RWKV_HEADS, RWKV_HEAD), 0.1)
    inp['rwkv_ln_w'] = 1.0 + nrm((E,), 0.02)
    inp['rwkv_ln_b'] = nrm((E,), 0.02)
    inp['rwkv_w_out'] = nrm((E, D_MODEL), E ** -0.5)
    inp['ret_w_in'] = nrm((D_MODEL, RET_IN), D_MODEL ** -0.5)
    inp['ret_w_out'] = nrm((E, D_MODEL), E ** -0.5)
    inp['ssd_w_in'] = nrm((D_MODEL, SSD_IN), D_MODEL ** -0.5)
    inp['ssd_conv_w'] = nrm((SSD_CONV, SSD_CONV_DIM), SSD_CONV ** -0.5)
    inp['ssd_conv_b'] = nrm((SSD_CONV_DIM,), 0.02)
    inp['ssd_dt_bias'] = dt0 + jnp.log(-jnp.expm1(-dt0))
    inp['ssd_A_log'] = jnp.log(unif((SSD_HEADS,), 1.0, 16.0))
    inp['ssd_D'] = 1.0 + nrm((SSD_HEADS,), 0.02)
    inp['ssd_norm_w'] = 1.0 + nrm((E,), 0.02)
    inp['ssd_w_out'] = nrm((E, D_MODEL), E ** -0.5)
    inp['gla_w_in'] = nrm((D_MODEL, GLA_IN), D_MODEL ** -0.5)
    inp['gla_a_up'] = nrm((GLA_LORA, GLA_KEY), GLA_LORA ** -0.5)
    inp['gla_a_bias'] = nrm((GLA_KEY,), 0.1)
    inp['gla_norm_w'] = 1.0 + nrm((GLA_DV,), 0.02)
    inp['gla_w_out'] = nrm((E, D_MODEL), E ** -0.5)
    return inp


def reference(x_prompt, x_sample, state_rwkv_shift, state_rwkv_wkv, state_ret, state_ssd_conv, state_ssd,
              state_gla, norm_gains, final_norm, rwkv_w_in, rwkv_mu, rwkv_w0, rwkv_w_up, rwkv_a0, rwkv_a_up,
              rwkv_k_k, rwkv_k_a, rwkv_r_k, rwkv_ln_w, rwkv_ln_b, rwkv_w_out, ret_w_in, ret_w_out, ssd_w_in,
              ssd_conv_w, ssd_conv_b, ssd_dt_bias, ssd_A_log, ssd_D, ssd_norm_w, ssd_w_out, gla_w_in, gla_a_up,
              gla_a_bias, gla_norm_w, gla_w_out):
    rwkv_p = (rwkv_w_in, rwkv_mu, rwkv_w0, rwkv_w_up, rwkv_a0, rwkv_a_up, rwkv_k_k, rwkv_k_a, rwkv_r_k,
              rwkv_ln_w, rwkv_ln_b, rwkv_w_out)
    ret_p = (ret_w_in, ret_w_out)
    ssd_p = (ssd_w_in, ssd_conv_w, ssd_conv_b, ssd_dt_bias, ssd_A_log, ssd_D, ssd_norm_w, ssd_w_out)
    gla_p = (gla_w_in, gla_a_up, gla_a_bias, gla_norm_w, gla_w_out)

    Bp, Tp = x_prompt.shape[0], x_prompt.shape[1]
    dt = x_prompt.dtype
    init_prompt = (jnp.zeros((Bp, D_MODEL), dt),
                   jnp.zeros((Bp, RWKV_HEADS, RWKV_HEAD, RWKV_HEAD), dt),
                   jnp.zeros((Bp, RET_HEADS, RET_DK, RET_DV), dt),
                   jnp.zeros((Bp, SSD_CONV - 1, SSD_CONV_DIM), dt),
                   jnp.zeros((Bp, SSD_HEADS, SSD_STATE, SSD_HEADDIM), dt),
                   jnp.zeros((Bp, GLA_HEADS, GLA_DK, GLA_DV), dt))
    y_prompt, (p_shift, p_wkv, p_ret, p_conv, p_ssd, p_gla) = trunk(
        x_prompt, jnp.arange(Tp), init_prompt, norm_gains, final_norm, rwkv_p, ret_p, ssd_p, gla_p)

    init_sample = (state_rwkv_shift, state_rwkv_wkv, state_ret, state_ssd_conv, state_ssd, state_gla)
    y_sample, (s_shift, s_wkv, s_ret, s_conv, s_ssd, s_gla) = trunk(
        x_sample, PAST_LEN + jnp.arange(x_sample.shape[1]), init_sample, norm_gains, final_norm,
        rwkv_p, ret_p, ssd_p, gla_p)

    return (y_prompt, y_sample, p_shift, p_wkv, p_ret, p_conv, p_ssd, p_gla,
            s_shift, s_wkv, s_ret, s_conv, s_ssd, s_gla)
```

```python
import functools
import math

import jax
import jax.numpy as jnp
from jax import lax
from jax.experimental import pallas as pl
from jax.experimental.pallas import tpu as pltpu

F32 = jnp.float32
BF16 = jnp.bfloat16

D_MODEL = 2048
D_INNER = 2 * D_MODEL
NORM_EPS = 1e-6
PAST_LEN = 16384

RWKV_HEAD = 64
RWKV_HEADS = D_INNER // RWKV_HEAD
RWKV_LORA = 96
RWKV_GN_EPS = 64e-5

RET_HEADS = 8
RET_DK = D_MODEL // RET_HEADS
RET_DV = D_INNER // RET_HEADS
ROPE_BASE = 10000.0

SSD_HEADDIM = 64
SSD_HEADS = D_INNER // SSD_HEADDIM
SSD_STATE = 128
SSD_GROUPS = 8
SSD_RPG = SSD_HEADS // SSD_GROUPS
SSD_CONV = 4
SSD_NORM_EPS = 1e-5
SSD_GW = D_INNER // SSD_GROUPS

GLA_HEADS = 4
GLA_KEY = D_MODEL // 2
GLA_DK = GLA_KEY // GLA_HEADS
GLA_DV = D_INNER // GLA_HEADS
GLA_LORA = 16
GLA_LOGIT_NORM = 16.0

LANES = 128
VMEM_LIMIT = 52 * 1024 * 1024


def _params(sem):
    return pltpu.CompilerParams(dimension_semantics=sem, vmem_limit_bytes=VMEM_LIMIT)


def _sigmoid(x):
    return 1.0 / (1.0 + jnp.exp(-x))


def _silu(x):
    return x * _sigmoid(x)


def _softplus(x):
    return jnp.maximum(x, 0.0) + jnp.log1p(jnp.exp(-jnp.abs(x)))


def _bdot(a, b):
    return jnp.dot(a.astype(BF16), b.astype(BF16), preferred_element_type=F32)


def _bdot_nt(a, b):
    return lax.dot_general(a.astype(BF16), b.astype(BF16), (((1,), (1,)), ((), ())),
                           preferred_element_type=F32)


def _bdot_tn(a, b):
    return lax.dot_general(a.astype(BF16), b.astype(BF16), (((0,), (0,)), ((), ())),
                           preferred_element_type=F32)


def _cumsum_rows(x):
    c = x.shape[0]
    row = lax.broadcasted_iota(jnp.int32, x.shape, 0)
    acc = jnp.zeros_like(x)
    for j in range(c):
        acc = acc + jnp.where(row >= j, x[j:j + 1, :], 0.0)
    return acc


def _norm_kernel(x_ref, g_ref, o_ref):
    x = x_ref[...]
    ms = jnp.mean(x * x, axis=-1, keepdims=True)
    o_ref[...] = (x * lax.rsqrt(ms + NORM_EPS) * g_ref[...]).astype(o_ref.dtype)


def _rmsnorm(x2d, gain, out_dtype):
    m, d = x2d.shape
    tm = 512 if m % 512 == 0 else m
    return pl.pallas_call(
        _norm_kernel,
        grid=(m // tm,),
        in_specs=[pl.BlockSpec((tm, d), lambda i: (i, 0)), pl.BlockSpec((1, d), lambda i: (0, 0))],
        out_specs=pl.BlockSpec((tm, d), lambda i: (i, 0)),
        out_shape=jax.ShapeDtypeStruct((m, d), out_dtype),
        compiler_params=_params(("parallel",)),
        name="rmsnorm",
    )(x2d, gain.reshape(1, d))


def _mm_kernel(a_ref, w_ref, *rest, has_res):
    if has_res:
        r_ref, o_ref, wb_ref = rest
    else:
        o_ref, wb_ref = rest

    @pl.when(pl.program_id(1) == 0)
    def _():
        wb_ref[...] = w_ref[...].astype(BF16)

    acc = jnp.dot(a_ref[...], wb_ref[...], preferred_element_type=F32)
    if has_res:
        acc = r_ref[...] + acc
    o_ref[...] = acc.astype(o_ref.dtype)


def _matmul(a, w, res=None, tn=1024):
    m, k = a.shape
    n = w.shape[1]
    tm = 512 if m % 512 == 0 else m
    tn = min(tn, n)
    grid = (pl.cdiv(n, tn), m // tm)
    in_specs = [pl.BlockSpec((tm, k), lambda j, i: (i, 0)), pl.BlockSpec((k, tn), lambda j, i: (0, j))]
    args = [a, w]
    if res is not None:
        in_specs.append(pl.BlockSpec((tm, tn), lambda j, i: (i, j)))
        args.append(res)
    return pl.pallas_call(
        functools.partial(_mm_kernel, has_res=res is not None),
        grid=grid,
        in_specs=in_specs,
        out_specs=pl.BlockSpec((tm, tn), lambda j, i: (i, j)),
        out_shape=jax.ShapeDtypeStruct((m, n), F32),
        scratch_shapes=[pltpu.VMEM((k, tn), BF16)],
        compiler_params=_params(("arbitrary", "arbitrary")),
        name="matmul",
    )(*args)


def _seg64_sum(x, ones_blk):
    hi = x.astype(BF16)
    lo = (x - hi.astype(F32)).astype(BF16)
    w = ones_blk.shape[0]
    cols = []
    for c in range(x.shape[1] // w):
        sl = slice(c * w, (c + 1) * w)
        cols.append(jnp.dot(hi[:, sl], ones_blk, preferred_element_type=F32)
                    + jnp.dot(lo[:, sl], ones_blk, preferred_element_type=F32))
    return jnp.concatenate(cols, axis=1)


def _shifted(buf_ref, carry_ref, x, c):
    buf_ref[7:8, :] = carry_ref[...]
    buf_ref[8:8 + c, :] = x
    prev = buf_ref[7:7 + c, :]
    carry_ref[...] = buf_ref[8 + c - 1:8 + c, :]
    return prev


def _rwkv_prep_kernel(pr_ref, pk_ref, pv_ref, pg_ref, pl_ref,
                      sr_ref, sk_ref, sv_ref, sg_ref, sl_ref,
                      mr_ref, mk_ref, mv_ref, mg_ref, ml_ref,
                      w0_ref, a0_ref, kk_ref, ka_ref, rk_ref, wup_ref, aup_ref, ones_ref,
                      r_out, w_out, k_out, v_out, a_out, b_out, bonus_out, sg_out,
                      buf_ref, bufl_ref, cr_ref, ck_ref, cv_ref, cg_ref, cl_ref, *, c):
    @pl.when(pl.program_id(1) == 0)
    def _():
        cr_ref[...] = sr_ref[0]
        ck_ref[...] = sk_ref[0]
        cv_ref[...] = sv_ref[0]
        cg_ref[...] = sg_ref[0]
        cl_ref[...] = sl_ref[0]

    def lerp(p_ref, carry_ref, mu_ref, buf):
        p = p_ref[0]
        prev = _shifted(buf, carry_ref, p, c)
        return p + mu_ref[...] * (prev - p)

    r = lerp(pr_ref, cr_ref, mr_ref, buf_ref)
    k = lerp(pk_ref, ck_ref, mk_ref, buf_ref)
    v = lerp(pv_ref, cv_ref, mv_ref, buf_ref)
    g = lerp(pg_ref, cg_ref, mg_ref, buf_ref)
    lo = lerp(pl_ref, cl_ref, ml_ref, bufl_ref)
    lane = lax.broadcasted_iota(jnp.int32, lo.shape, 1)
    lo = jnp.where(lane < 2 * RWKV_LORA, lo, 0.0)

    w_raw = w0_ref[...] + _bdot(jnp.tanh(lo), wup_ref[...])
    decay = jnp.exp(-jnp.exp(-_softplus(-w_raw) - 0.5))
    a = _sigmoid(a0_ref[...] + _bdot(lo, aup_ref[...]))
    ones_blk = ones_ref[...]
    kk32 = k * kk_ref[...]
    nrm = jnp.sqrt(_seg64_sum(kk32 * kk32, ones_blk))
    kk = kk32 / jnp.maximum(nrm, 1e-12)
    k2 = k * (1.0 + (a - 1.0) * ka_ref[...])
    bonus = _seg64_sum(r * k2 * rk_ref[...], ones_blk) * v

    r_out[0] = r
    w_out[0] = decay
    k_out[0] = k2
    v_out[0] = v
    a_out[0] = -kk
    b_out[0] = kk * a
    bonus_out[0] = bonus
    sg_out[0] = _silu(g)


def _rwkv_prep(p, p0, mu, w0, a0, k_k, k_a, r_k, w_up, a_up):
    b, t, n_in = p.shape
    c = min(t, 64)
    e = D_INNER
    lw = 2 * LANES
    lblk = (4 * e) // lw
    row = lambda x: x.reshape(1, -1)
    wup = jnp.zeros((lw, e), F32).at[:RWKV_LORA].set(w_up)
    aup = jnp.zeros((lw, e), F32).at[RWKV_LORA:2 * RWKV_LORA].set(a_up)
    seg = jnp.arange(lw) // RWKV_HEAD
    ones_blk = (seg[:, None] == seg[None, :]).astype(BF16)
    big = lambda j: pl.BlockSpec((1, c, e), lambda bi, ci, j=j: (bi, ci, j))
    big0 = lambda j: pl.BlockSpec((1, 1, e), lambda bi, ci, j=j: (bi, 0, j))
    mub = lambda j: pl.BlockSpec((1, e), lambda bi, ci, j=j: (0, j))
    vec = pl.BlockSpec((1, e), lambda bi, ci: (0, 0))
    full = lambda shp: pl.BlockSpec(shp, lambda bi, ci: (0, 0))
    mu2 = row(mu)
    in_specs = ([big(j) for j in range(4)] + [pl.BlockSpec((1, c, lw), lambda bi, ci: (bi, ci, lblk))]
                + [big0(j) for j in range(4)] + [pl.BlockSpec((1, 1, lw), lambda bi, ci: (bi, 0, lblk))]
                + [mub(j) for j in range(4)] + [pl.BlockSpec((1, lw), lambda bi, ci: (0, lblk))]
                + [vec] * 5 + [full((lw, e)), full((lw, e)), full((lw, lw))])
    out_spec = pl.BlockSpec((1, c, e), lambda bi, ci: (bi, ci, 0))
    outs = pl.pallas_call(
        functools.partial(_rwkv_prep_kernel, c=c),
        grid=(b, t // c),
        in_specs=in_specs,
        out_specs=[out_spec] * 8,
        out_shape=[jax.ShapeDtypeStruct((b, t, e), F32)] * 8,
        scratch_shapes=[pltpu.VMEM((8 + c, e), F32), pltpu.VMEM((8 + c, lw), F32)]
        + [pltpu.VMEM((1, e), F32)] * 4 + [pltpu.VMEM((1, lw), F32)],
        compiler_params=_params(("parallel", "arbitrary")),
        name="rwkv_prep",
    )(p, p, p, p, p, p0, p0, p0, p0, p0, mu2, mu2, mu2, mu2, mu2,
      row(w0), row(a0), row(k_k), row(k_a), row(r_k), wup, aup, ones_blk)
    return outs


def _rwkv_scan_kernel(r_ref, w_ref, k_ref, v_ref, a_ref, b_ref, s0_ref, y_ref, s_out_ref, st_ref, *, c):
    n = RWKV_HEAD

    @pl.when(pl.program_id(1) == 0)
    def _():
        st_ref[...] = s0_ref[...]

    def step(t, carry):
        def pass1(kq, sa):
            return sa + st_ref[kq] * a_ref[t, pl.ds(kq, 1), :]
        sa = lax.fori_loop(0, n, pass1, jnp.zeros((n, LANES), F32), unroll=8)
        vt = v_ref[t]

        def pass2(kq, y):
            s_new = (st_ref[kq] * w_ref[t, pl.ds(kq, 1), :] + sa * b_ref[t, pl.ds(kq, 1), :]
                     + vt * k_ref[t, pl.ds(kq, 1), :])
            st_ref[kq] = s_new
            return y + s_new * r_ref[t, pl.ds(kq, 1), :]
        y_ref[t] = lax.fori_loop(0, n, pass2, jnp.zeros((n, LANES), F32), unroll=8)
        return carry

    lax.fori_loop(0, c, step, 0)

    @pl.when(pl.program_id(1) == pl.num_programs(1) - 1)
    def _():
        s_out_ref[...] = st_ref[...]


def _rwkv_scan(r, w, k, v, a, b, s0):
    t, n, l = r.shape
    c = min(t, 32)
    tok = pl.BlockSpec((c, n, LANES), lambda g, ci: (ci, 0, g))
    st = pl.BlockSpec((n, n, LANES), lambda g, ci: (0, 0, g))
    return pl.pallas_call(
        functools.partial(_rwkv_scan_kernel, c=c),
        grid=(l // LANES, t // c),
        in_specs=[tok] * 6 + [st],
        out_specs=[tok, st],
        out_shape=[jax.ShapeDtypeStruct((t, n, l), F32), jax.ShapeDtypeStruct((n, n, l), F32)],
        scratch_shapes=[pltpu.VMEM((n, n, LANES), F32)],
        compiler_params=_params(("parallel", "arbitrary")),
        name="rwkv_scan",
    )(r, w, k, v, a, b, s0)


def _rwkv_post_kernel(y_ref, bonus_ref, sg_ref, lnw_ref, lnb_ref, ones_ref, h_ref):
    y = y_ref[0]
    ones_blk = ones_ref[...]
    inv = 1.0 / RWKV_HEAD
    mean = _seg64_sum(y, ones_blk) * inv
    d = y - mean
    var = _seg64_sum(d * d, ones_blk) * inv
    yn = d * lax.rsqrt(var + RWKV_GN_EPS) * lnw_ref[...] + lnb_ref[...]
    h_ref[0] = ((yn + bonus_ref[0]) * sg_ref[0]).astype(h_ref.dtype)


def _rwkv_post(y, bonus, sg, ln_w, ln_b):
    b, t, e = y.shape
    c = min(t, 256)
    lw = 2 * LANES
    seg = jnp.arange(lw) // RWKV_HEAD
    ones_blk = (seg[:, None] == seg[None, :]).astype(BF16)
    tok = pl.BlockSpec((1, c, e), lambda bi, ci: (bi, ci, 0))
    vec = pl.BlockSpec((1, e), lambda bi, ci: (0, 0))
    return pl.pallas_call(
        _rwkv_post_kernel,
        grid=(b, t // c),
        in_specs=[tok, tok, tok, vec, vec, pl.BlockSpec((lw, lw), lambda bi, ci: (0, 0))],
        out_specs=tok,
        out_shape=jax.ShapeDtypeStruct((b, t, e), BF16),
        compiler_params=_params(("parallel", "parallel")),
        name="rwkv_post",
    )(y, bonus, sg, ln_w.reshape(1, e), ln_b.reshape(1, e), ones_blk)


def _rwkv_mixer(xn, shift, wkv, w_in, mu, w0, w_up, a0, a_up, k_k, k_a, r_k, ln_w, ln_b):
    b, t, d = xn.shape
    hh, n = RWKV_HEADS, RWKV_HEAD
    p = _matmul(xn.reshape(b * t, d), w_in).reshape(b, t, -1)
    if shift is None:
        p0 = jnp.zeros((b, 1, p.shape[-1]), F32)
    else:
        p0 = _matmul(shift.astype(BF16), w_in).reshape(b, 1, -1)
    r, w, k, v, a, bb, bonus, sg = _rwkv_prep(p, p0, mu, w0, a0, k_k, k_a, r_k, w_up, a_up)
    lanes_first = lambda x: x.reshape(b, t, hh, n).transpose(1, 3, 0, 2).reshape(t, n, b * hh)
    s0 = wkv.transpose(3, 2, 0, 1).reshape(n, n, b * hh)
    y, s1 = _rwkv_scan(*(lanes_first(x) for x in (r, w, k, v, a, bb)), s0)
    y = y.reshape(t, n, b, hh).transpose(2, 0, 3, 1).reshape(b, t, hh * n)
    wkv_new = s1.reshape(n, n, b, hh).transpose(2, 3, 1, 0)
    return _rwkv_post(y, bonus, sg, ln_w, ln_b), wkv_new


def _ret_kernel(lg_ref, q_ref, k_ref, v_ref, g_ref, cos_ref, sin_ref, s0_ref, h_ref, s_out_ref, s_ref, *, c):
    hd = pl.program_id(1)

    @pl.when(pl.program_id(2) == 0)
    def _():
        s_ref[...] = s0_ref[0, 0]

    lg = lg_ref[hd]
    cos = cos_ref[...]
    sin = sin_ref[...]
    half = RET_DK // 2

    def rot(x):
        x1, x2 = x[:, :half], x[:, half:]
        return jnp.concatenate([x1 * cos - x2 * sin, x1 * sin + x2 * cos], axis=1)

    q = rot(q_ref[0])
    k = rot(k_ref[0]) * (RET_DK ** -0.5)
    v = v_ref[0]
    ti = lax.broadcasted_iota(jnp.int32, (c, c), 0)
    si = lax.broadcasted_iota(jnp.int32, (c, c), 1)
    dmat = jnp.exp(jnp.where(ti >= si, (ti - si).astype(F32) * lg, -jnp.inf))
    trow = lax.broadcasted_iota(jnp.int32, (c, RET_DK), 0).astype(F32)
    s = s_ref[...]
    att = _bdot_nt(q, k) * dmat
    y = _bdot(att, v) + _bdot(q * jnp.exp((trow + 1.0) * lg), s)
    s_ref[...] = jnp.exp(jnp.zeros((1, RET_DV), F32) + lg * c) * s + _bdot_tn(k * jnp.exp((c - 1.0 - trow) * lg), v)

    yn = y * lax.rsqrt(jnp.mean(y * y, axis=-1, keepdims=True) + NORM_EPS)
    h_ref[0] = (yn * _silu(g_ref[0])).astype(h_ref.dtype)

    @pl.when(pl.program_id(2) == pl.num_programs(2) - 1)
    def _():
        s_out_ref[0, 0] = s_ref[...]


def _ret_mixer(xn, pos, s0, w_in):
    b, t, d = xn.shape
    p = _matmul(xn.reshape(b * t, d), w_in).reshape(b, t, -1)
    c = min(t, 256)
    half = RET_DK // 2
    inv = 1.0 / (ROPE_BASE ** jnp.linspace(0.0, 1.0, half, dtype=F32))
    ang = pos.astype(F32)[:, None] * inv[None, :]
    cos, sin = jnp.cos(ang), jnp.sin(ang)
    log_gamma = jnp.log1p(-(2.0 ** (-5.0 - jnp.arange(RET_HEADS, dtype=F32))))
    nq = D_MODEL // RET_DK
    spec = lambda w, off: pl.BlockSpec((1, c, w), lambda bi, hi, ci, lg, off=off: (bi, ci, off + hi))
    tab = pl.BlockSpec((c, half), lambda bi, hi, ci, lg: (ci, 0))
    st = pl.BlockSpec((1, 1, RET_DK, RET_DV), lambda bi, hi, ci, lg: (bi, hi, 0, 0))
    h, s1 = pl.pallas_call(
        functools.partial(_ret_kernel, c=c),
        grid_spec=pltpu.PrefetchScalarGridSpec(
            num_scalar_prefetch=1,
            grid=(b, RET_HEADS, t // c),
            in_specs=[spec(RET_DK, 0), spec(RET_DK, nq), spec(RET_DV, (2 * D_MODEL) // RET_DV),
                      spec(RET_DV, (2 * D_MODEL + D_INNER) // RET_DV), tab, tab, st],
            out_specs=[pl.BlockSpec((1, c, RET_DV), lambda bi, hi, ci, lg: (bi, ci, hi)), st],
            scratch_shapes=[pltpu.VMEM((RET_DK, RET_DV), F32)],
        ),
        out_shape=[jax.ShapeDtypeStruct((b, t, D_INNER), BF16), jax.ShapeDtypeStruct(s0.shape, F32)],
        compiler_params=_params(("parallel", "parallel", "arbitrary")),
        name="retention",
    )(log_gamma, p, p, p, p, cos, sin, s0)
    return h, s1


def _causal_conv(buf_ref, x, w, bias, c):
    buf_ref[8:8 + c, :] = x
    acc = bias
    for j in range(SSD_CONV):
        acc = acc + w[j:j + 1, :] * buf_ref[5 + j:5 + j + c, :]
    buf_ref[5:8, :] = buf_ref[5 + c:8 + c, :]
    return acc


def _ssd_kernel(z_ref, x_ref, bm_ref, cm_ref, dt_ref, cx_ref, cb_ref, cc_ref,
                wx_ref, wb_ref, wc_ref, bx_ref, bb_ref, bc_ref, dtb_ref, alog_ref, dsk_ref, nw_ref, s0_ref,
                h_ref, s_out_ref, bufx_ref, bufb_ref, bufc_ref, s_ref, *, c):
    g = pl.program_id(1)

    @pl.when(pl.program_id(2) == 0)
    def _():
        s_ref[...] = s0_ref[0]
        bufx_ref[5:8, :] = cx_ref[0]
        bufb_ref[5:8, :] = cb_ref[0]
        bufc_ref[5:8, :] = cc_ref[0]

    xs = _silu(_causal_conv(bufx_ref, x_ref[0], wx_ref[...], bx_ref[...], c))
    bm = _silu(_causal_conv(bufb_ref, bm_ref[0], wb_ref[...], bb_ref[...], c))
    cm = _silu(_causal_conv(bufc_ref, cm_ref[0], wc_ref[...], bc_ref[...], c))

    lane = lax.broadcasted_iota(jnp.int32, (c, LANES), 1)
    dt = jnp.where(lane < SSD_HEADS, _softplus(dt_ref[0] + dtb_ref[...]), 0.0)
    la = dt * (-jnp.exp(alog_ref[...]))
    lcum = _cumsum_rows(la)

    scores = _bdot_nt(cm, bm)
    ti = lax.broadcasted_iota(jnp.int32, (c, c), 0)
    si = lax.broadcasted_iota(jnp.int32, (c, c), 1)
    causal = ti >= si
    eye = ti == si
    last = lax.broadcasted_iota(jnp.int32, (c, 1), 0) == c - 1
    ys = []
    for r in range(SSD_RPG):
        head = g * SSD_RPG + r
        sel = lane == head
        lcol = jnp.sum(jnp.where(sel, lcum, 0.0), axis=-1, keepdims=True)
        dtc = jnp.sum(jnp.where(sel, dt, 0.0), axis=-1, keepdims=True)
        lrow = jnp.sum(jnp.where(eye, lcol, 0.0), axis=0, keepdims=True)
        lend = jnp.sum(jnp.where(last, lcol, 0.0), axis=0, keepdims=True)
        dec = jnp.exp(jnp.where(causal, lcol - lrow, -jnp.inf))
        x_r = xs[:, r * SSD_HEADDIM:(r + 1) * SSD_HEADDIM]
        v_r = x_r * dtc
        s_r = s_ref[r]
        y_r = _bdot(scores * dec, v_r) + jnp.exp(lcol) * _bdot(cm, s_r)
        s_ref[r] = jnp.exp(lend) * s_r + _bdot_tn(bm, jnp.exp(lend - lcol) * v_r)
        ys.append(y_r)
    y = jnp.concatenate(ys, axis=1) + xs * dsk_ref[...]
    y = y * _silu(z_ref[0])
    yn = y * lax.rsqrt(jnp.mean(y * y, axis=-1, keepdims=True) + SSD_NORM_EPS)
    h_ref[0] = (yn * nw_ref[...]).astype(h_ref.dtype)

    @pl.when(pl.program_id(2) == pl.num_programs(2) - 1)
    def _():
        s_out_ref[0] = s_ref[...]


def _ssd_mixer(xn, conv_state, s0, w_in, conv_w, conv_b, dt_bias, a_log, d_skip, norm_w):
    b, t, d = xn.shape
    p = _matmul(xn.reshape(b * t, d), w_in).reshape(b, t, -1)
    c = min(t, 64)
    e, gw, st = D_INNER, SSD_GW, SSD_STATE
    ng = SSD_GROUPS
    xoff = e // gw
    boff = (2 * e) // st
    coff = boff + ng
    dtoff = (2 * e + 2 * ng * st) // LANES
    tokw = lambda w, off: pl.BlockSpec((1, c, w), lambda bi, gi, ci, off=off: (bi, ci, off + gi))
    cs = lambda w, off: pl.BlockSpec((1, SSD_CONV - 1, w), lambda bi, gi, ci, off=off: (bi, 0, off + gi))
    cw = lambda w, off: pl.BlockSpec((SSD_CONV, w), lambda bi, gi, ci, off=off: (0, off + gi))
    cbv = lambda w, off: pl.BlockSpec((1, w), lambda bi, gi, ci, off=off: (0, off + gi))
    hv = pl.BlockSpec((1, LANES), lambda bi, gi, ci: (0, 0))
    pad_heads = lambda x: jnp.zeros((1, LANES), F32).at[0, :SSD_HEADS].set(x)
    cb2 = conv_b.reshape(1, -1)
    stspec = pl.BlockSpec((1, SSD_RPG, st, SSD_HEADDIM), lambda bi, gi, ci: (bi, gi, 0, 0))
    h, s1 = pl.pallas_call(
        functools.partial(_ssd_kernel, c=c),
        grid=(b, ng, t // c),
        in_specs=[tokw(gw, 0), tokw(gw, xoff), tokw(st, boff), tokw(st, coff),
                  pl.BlockSpec((1, c, LANES), lambda bi, gi, ci: (bi, ci, dtoff)),
                  cs(gw, 0), cs(st, e // st), cs(st, e // st + ng),
                  cw(gw, 0), cw(st, e // st), cw(st, e // st + ng),
                  cbv(gw, 0), cbv(st, e // st), cbv(st, e // st + ng),
                  hv, hv, cbv(gw, 0), cbv(gw, 0), stspec],
        out_specs=[tokw(gw, 0), stspec],
        out_shape=[jax.ShapeDtypeStruct((b, t, e), BF16), jax.ShapeDtypeStruct(s0.shape, F32)],
        scratch_shapes=[pltpu.VMEM((8 + c, gw), F32), pltpu.VMEM((8 + c, st), F32), pltpu.VMEM((8 + c, st), F32),
                        pltpu.VMEM((SSD_RPG, st, SSD_HEADDIM), F32)],
        compiler_params=_params(("parallel", "parallel", "arbitrary")),
        name="ssd",
    )(p, p, p, p, p, conv_state, conv_state, conv_state, conv_w, conv_w, conv_w, cb2, cb2, cb2,
      pad_heads(dt_bias), pad_heads(a_log), jnp.repeat(d_skip, SSD_HEADDIM).reshape(1, e),
      norm_w.reshape(1, e), s0)
    conv_new = p[:, t - (SSD_CONV - 1):, e:e + e + 2 * ng * st]
    return h, conv_new, s1


def _gla_kernel(q_ref, k_ref, v_ref, g_ref, ad_ref, aup_ref, ab_ref, nw_ref, s0_ref, h_ref, s_out_ref, s_ref, *, c):
    @pl.when(pl.program_id(2) == 0)
    def _():
        s_ref[...] = s0_ref[0, 0]

    lane = lax.broadcasted_iota(jnp.int32, (c, LANES), 1)
    ad = jnp.where(lane < GLA_LORA, ad_ref[0], 0.0)
    la = -_softplus(-(_bdot(ad, aup_ref[...]) + ab_ref[...])) / GLA_LOGIT_NORM
    lcum = _cumsum_rows(la)
    q = q_ref[0] * (GLA_DK ** -0.5)
    k = k_ref[0]
    v = v_ref[0]
    qe = q * jnp.exp(lcum)
    ke = k * jnp.exp(-lcum)
    ti = lax.broadcasted_iota(jnp.int32, (c, c), 0)
    si = lax.broadcasted_iota(jnp.int32, (c, c), 1)
    att = jnp.where(ti >= si, _bdot_nt(qe, ke), 0.0)
    rowi = lax.broadcasted_iota(jnp.int32, (c, GLA_DK), 0)
    lend = jnp.sum(jnp.where(rowi == c - 1, lcum, 0.0), axis=0, keepdims=True)
    kend = k * jnp.exp(lend - lcum)
    s = s_ref[...]
    y = _bdot(att, v) + _bdot(qe, s)
    eye = (lax.broadcasted_iota(jnp.int32, (GLA_DK, GLA_DK), 0)
           == lax.broadcasted_iota(jnp.int32, (GLA_DK, GLA_DK), 1))
    gend_col = jnp.sum(jnp.where(eye, jnp.exp(lend), 0.0), axis=1, keepdims=True)
    s_ref[...] = gend_col * s + _bdot_tn(kend, v)

    yn = y * lax.rsqrt(jnp.mean(y * y, axis=-1, keepdims=True) + NORM_EPS) * nw_ref[...]
    h_ref[0] = (yn * _silu(g_ref[0])).astype(h_ref.dtype)

    @pl.when(pl.program_id(2) == pl.num_programs(2) - 1)
    def _():
        s_out_ref[0, 0] = s_ref[...]


def _gla_mixer(xn, s0, w_in, a_up, a_bias, norm_w):
    b, t, d = xn.shape
    p = _matmul(xn.reshape(b * t, d), w_in).reshape(b, t, -1)
    c = min(t, 16)
    dk, dv = GLA_DK, GLA_DV
    tokw = lambda w, off: pl.BlockSpec((1, c, w), lambda bi, hi, ci, off=off: (bi, ci, off + hi))
    aup = jnp.zeros((LANES, GLA_KEY), F32).at[:GLA_LORA].set(a_up)
    st = pl.BlockSpec((1, 1, dk, dv), lambda bi, hi, ci: (bi, hi, 0, 0))
    h, s1 = pl.pallas_call(
        functools.partial(_gla_kernel, c=c),
        grid=(b, GLA_HEADS, t // c),
        in_specs=[tokw(dk, 0), tokw(dk, GLA_HEADS), tokw(dv, (2 * GLA_KEY) // dv),
                  tokw(dv, (2 * GLA_KEY + D_INNER) // dv),
                  pl.BlockSpec((1, c, LANES), lambda bi, hi, ci: (bi, ci, (2 * GLA_KEY + 2 * D_INNER) // LANES)),
                  pl.BlockSpec((LANES, dk), lambda bi, hi, ci: (0, hi)),
                  pl.BlockSpec((1, dk), lambda bi, hi, ci: (0, hi)),
                  pl.BlockSpec((1, dv), lambda bi, hi, ci: (0, 0)), st],
        out_specs=[tokw(dv, 0), st],
        out_shape=[jax.ShapeDtypeStruct((b, t, D_INNER), BF16), jax.ShapeDtypeStruct(s0.shape, F32)],
        scratch_shapes=[pltpu.VMEM((dk, dv), F32)],
        compiler_params=_params(("parallel", "parallel", "arbitrary")),
        name="gla",
    )(p, p, p, p, p, aup, a_bias.reshape(1, -1), norm_w.reshape(1, -1), s0)
    return h, s1


def _trunk(x, pos, states, norm_gains, final_norm, rwkv_p, ret_p, ssd_p, gla_p):
    shift, wkv, ret_s, conv_s, ssd_s, gla_s = states
    b, t, d = x.shape
    x2 = x.reshape(b * t, d)

    def normed(i):
        return _rmsnorm(x2, norm_gains[i], BF16).reshape(b, t, d)

    (w_in, mu, w0, w_up, a0, a_up, k_k, k_a, r_k, ln_w, ln_b, w_out) = rwkv_p
    shift_new = _rmsnorm(x[:, -1], norm_gains[0], F32)
    h, wkv = _rwkv_mixer(normed(0), shift, wkv, w_in, mu, w0, w_up, a0, a_up, k_k, k_a, r_k, ln_w, ln_b)
    x2 = _matmul(h.reshape(b * t, -1), w_out, res=x2, tn=512)
    w_in, w_out = ret_p
    h, ret_s = _ret_mixer(normed(1), pos, ret_s, w_in)
    x2 = _matmul(h.reshape(b * t, -1), w_out, res=x2, tn=512)
    (w_in, conv_w, conv_b, dt_bias, a_log, d_skip, norm_w, w_out) = ssd_p
    h, conv_s, ssd_s = _ssd_mixer(normed(2), conv_s, ssd_s, w_in, conv_w, conv_b, dt_bias, a_log, d_skip, norm_w)
    x2 = _matmul(h.reshape(b * t, -1), w_out, res=x2, tn=512)
    (w_in, a_up, a_bias, norm_w, w_out) = gla_p
    h, gla_s = _gla_mixer(normed(3), gla_s, w_in, a_up, a_bias, norm_w)
    x2 = _matmul(h.reshape(b * t, -1), w_out, res=x2, tn=512)

    y = _rmsnorm(x2, final_norm, F32).reshape(b, t, d)
    return y, (shift_new, wkv, ret_s, conv_s, ssd_s, gla_s)


def kernel(x_prompt, x_sample, state_rwkv_shift, state_rwkv_wkv, state_ret, state_ssd_conv, state_ssd, state_gla, norm_gains, final_norm, rwkv_w_in, rwkv_mu, rwkv_w0, rwkv_w_up, rwkv_a0, rwkv_a_up, rwkv_k_k, rwkv_k_a, rwkv_r_k, rwkv_ln_w, rwkv_ln_b, rwkv_w_out, ret_w_in, ret_w_out, ssd_w_in, ssd_conv_w, ssd_conv_b, ssd_dt_bias, ssd_A_log, ssd_D, ssd_norm_w, ssd_w_out, gla_w_in, gla_a_up, gla_a_bias, gla_norm_w, gla_w_out):
    rwkv_p = (rwkv_w_in, rwkv_mu, rwkv_w0, rwkv_w_up, rwkv_a0, rwkv_a_up, rwkv_k_k, rwkv_k_a, rwkv_r_k,
              rwkv_ln_w, rwkv_ln_b, rwkv_w_out)
    ret_p = (ret_w_in, ret_w_out)
    ssd_p = (ssd_w_in, ssd_conv_w, ssd_conv_b, ssd_dt_bias, ssd_A_log, ssd_D, ssd_norm_w, ssd_w_out)
    gla_p = (gla_w_in, gla_a_up, gla_a_bias, gla_norm_w, gla_w_out)

    bp, tp = x_prompt.shape[0], x_prompt.shape[1]
    zeros = lambda ref: jnp.zeros((bp,) + ref.shape[1:], F32)
    init_prompt = (None, zeros(state_rwkv_wkv), zeros(state_ret), zeros(state_ssd_conv),
                   zeros(state_ssd), zeros(state_gla))
    y_p, st_p = _trunk(x_prompt, jnp.arange(tp), init_prompt, norm_gains, final_norm, rwkv_p, ret_p, ssd_p, gla_p)
    init_sample = (state_rwkv_shift, state_rwkv_wkv, state_ret, state_ssd_conv, state_ssd, state_gla)
    y_s, st_s = _trunk(x_sample, PAST_LEN + jnp.arange(x_sample.shape[1]), init_sample, norm_gains, final_norm,
                       rwkv_p, ret_p, ssd_p, gla_p)
    return (y_p, y_s) + tuple(st_p) + tuple(st_s)
```

```python
import functools

import jax
import jax.numpy as jnp
from jax import lax
from jax.experimental import pallas as pl
from jax.experimental.pallas import tpu as pltpu

F32 = jnp.float32
BF16 = jnp.bfloat16

D_MODEL = 2048
D_INNER = 2 * D_MODEL
NORM_EPS = 1e-6
PAST_LEN = 16384

RWKV_HEAD = 64
RWKV_HEADS = D_INNER // RWKV_HEAD
RWKV_LORA = 96
RWKV_GN_EPS = 64e-5

RET_HEADS = 8
RET_DK = D_MODEL // RET_HEADS
RET_DV = D_INNER // RET_HEADS
ROPE_BASE = 10000.0

SSD_HEADDIM = 64
SSD_HEADS = D_INNER // SSD_HEADDIM
SSD_STATE = 128
SSD_GROUPS = 8
SSD_RPG = SSD_HEADS // SSD_GROUPS
SSD_CONV = 4
SSD_NORM_EPS = 1e-5
SSD_GW = D_INNER // SSD_GROUPS

GLA_HEADS = 4
GLA_KEY = D_MODEL // 2
GLA_DK = GLA_KEY // GLA_HEADS
GLA_DV = D_INNER // GLA_HEADS
GLA_LORA = 16
GLA_LOGIT_NORM = 16.0

LANES = 128
VMEM_LIMIT = 52 * 1024 * 1024


def _params(sem):
    return pltpu.CompilerParams(dimension_semantics=sem, vmem_limit_bytes=VMEM_LIMIT)


def _sigmoid(x):
    return 1.0 / (1.0 + jnp.exp(-x))


def _silu(x):
    return x * _sigmoid(x)


def _softplus(x):
    return jnp.maximum(x, 0.0) + jnp.log1p(jnp.exp(-jnp.abs(x)))


def _bdot(a, b):
    return jnp.dot(a.astype(BF16), b.astype(BF16), preferred_element_type=F32)


def _bdot_nt(a, b):
    return lax.dot_general(a.astype(BF16), b.astype(BF16), (((1,), (1,)), ((), ())),
                           preferred_element_type=F32)


def _bdot_tn(a, b):
    return lax.dot_general(a.astype(BF16), b.astype(BF16), (((0,), (0,)), ((), ())),
                           preferred_element_type=F32)


def _cumsum_rows(x):
    c = x.shape[0]
    row = lax.broadcasted_iota(jnp.int32, x.shape, 0)
    acc = jnp.zeros_like(x)
    for j in range(c):
        acc = acc + jnp.where(row >= j, x[j:j + 1, :], 0.0)
    return acc


def _norm_kernel(x_ref, g_ref, o_ref):
    x = x_ref[...]
    ms = jnp.mean(x * x, axis=-1, keepdims=True)
    o_ref[...] = (x * lax.rsqrt(ms + NORM_EPS) * g_ref[...]).astype(o_ref.dtype)


def _rmsnorm(x2d, gain, out_dtype):
    m, d = x2d.shape
    tm = 512 if m % 512 == 0 else m
    return pl.pallas_call(
        _norm_kernel,
        grid=(m // tm,),
        in_specs=[pl.BlockSpec((tm, d), lambda i: (i, 0)), pl.BlockSpec((1, d), lambda i: (0, 0))],
        out_specs=pl.BlockSpec((tm, d), lambda i: (i, 0)),
        out_shape=jax.ShapeDtypeStruct((m, d), out_dtype),
        compiler_params=_params(("parallel",)),
        name="rmsnorm",
    )(x2d, gain.reshape(1, d))


def _mm_kernel(a_ref, w_ref, *rest, has_res):
    if has_res:
        r_ref, o_ref, wb_ref = rest
    else:
        o_ref, wb_ref = rest

    @pl.when(pl.program_id(1) == 0)
    def _():
        wb_ref[...] = w_ref[...].astype(BF16)

    acc = jnp.dot(a_ref[...], wb_ref[...], preferred_element_type=F32)
    if has_res:
        acc = r_ref[...] + acc
    o_ref[...] = acc.astype(o_ref.dtype)


def _matmul(a, w, res=None, tn=1024):
    m, k = a.shape
    n = w.shape[1]
    tm = 512 if m % 512 == 0 else m
    tn = min(tn, n)
    grid = (pl.cdiv(n, tn), m // tm)
    in_specs = [pl.BlockSpec((tm, k), lambda j, i: (i, 0)), pl.BlockSpec((k, tn), lambda j, i: (0, j))]
    args = [a, w]
    if res is not None:
        in_specs.append(pl.BlockSpec((tm, tn), lambda j, i: (i, j)))
        args.append(res)
    return pl.pallas_call(
        functools.partial(_mm_kernel, has_res=res is not None),
        grid=grid,
        in_specs=in_specs,
        out_specs=pl.BlockSpec((tm, tn), lambda j, i: (i, j)),
        out_shape=jax.ShapeDtypeStruct((m, n), F32),
        scratch_shapes=[pltpu.VMEM((k, tn), BF16)],
        compiler_params=_params(("arbitrary", "arbitrary")),
        name="matmul",
    )(*args)


def _shifted(buf_ref, carry_ref, x, c):
    buf_ref[7:8, :] = carry_ref[...]
    buf_ref[8:8 + c, :] = x
    prev = buf_ref[7:7 + c, :]
    carry_ref[...] = buf_ref[8 + c - 1:8 + c, :]
    return prev


def _rwkv_lora_kernel(p_ref, s_ref, mu_ref, wup_ref, aup_ref, w_out, a_out, buf_ref, carry_ref, *, c):
    @pl.when(pl.program_id(1) == 0)
    def _():
        carry_ref[...] = s_ref[0]

    p = p_ref[0]
    lo = p + mu_ref[...] * (_shifted(buf_ref, carry_ref, p, c) - p)
    w_out[0] = _bdot(jnp.tanh(lo), wup_ref[...])
    a_out[0] = _bdot(lo, aup_ref[...])


def _rwkv_lora(p, p0, mu, wup, aup):
    b, t, lw = p.shape
    e = wup.shape[1]
    c = min(t, 256)
    tok = pl.BlockSpec((1, c, e), lambda bi, ci: (bi, ci, 0))
    return pl.pallas_call(
        functools.partial(_rwkv_lora_kernel, c=c),
        grid=(b, t // c),
        in_specs=[pl.BlockSpec((1, c, lw), lambda bi, ci: (bi, ci, 0)),
                  pl.BlockSpec((1, 1, lw), lambda bi, ci: (bi, 0, 0)),
                  pl.BlockSpec((1, lw), lambda bi, ci: (0, 0)),
                  pl.BlockSpec((lw, e), lambda bi, ci: (0, 0)),
                  pl.BlockSpec((lw, e), lambda bi, ci: (0, 0))],
        out_specs=[tok, tok],
        out_shape=[jax.ShapeDtypeStruct((b, t, e), F32)] * 2,
        scratch_shapes=[pltpu.VMEM((8 + c, lw), F32), pltpu.VMEM((1, lw), F32)],
        compiler_params=_params(("parallel", "arbitrary")),
        name="rwkv_lora",
    )(p, p0, mu, wup, aup)


def _rwkv_core_kernel(r_ref, k_ref, v_ref, g_ref, wl_ref, al_ref, p0_ref, mu_ref, pv_ref, s0_ref,
                      h_ref, s_out_ref,
                      st_ref, carry_ref, rb, wb, kb, vb, ab, bb, bonb, sgb, yb, *, c):
    n = RWKV_HEAD

    @pl.when(pl.program_id(1) == 0)
    def _():
        st_ref[...] = s0_ref[...]
        carry_ref[...] = p0_ref[0]

    def pair(ref):
        if len(ref.shape) == 6:
            return jnp.concatenate([ref[0, 0, :, 0], ref[0, 1, :, 0]], axis=-1)
        return jnp.concatenate([ref[0, 0], ref[0, 1]], axis=-1)

    def lerp(i, ref):
        x = pair(ref)
        prev = jnp.concatenate([carry_ref[i][None], x[:c - 1]], axis=0)
        carry_ref[i] = x[c - 1]
        return x + mu_ref[i] * (prev - x)

    r = lerp(0, r_ref)
    k = lerp(1, k_ref)
    v = lerp(2, v_ref)
    g = lerp(3, g_ref)
    w0, a0, k_k, k_a, r_k, ln_w, ln_b = (pv_ref[i] for i in range(7))
    decay = jnp.exp(-jnp.exp(-_softplus(-(w0 + pair(wl_ref))) - 0.5))
    a = _sigmoid(a0 + pair(al_ref))
    kk32 = k * k_k
    nrm = jnp.sqrt(jnp.sum(kk32 * kk32, axis=1, keepdims=True))
    kk = kk32 / jnp.maximum(nrm, 1e-12)
    k2 = k * (1.0 + (a - 1.0) * k_a)
    rb[...] = r
    wb[...] = decay
    kb[...] = k2
    vb[...] = v
    ab[...] = -kk
    bb[...] = kk * a
    bonb[...] = jnp.sum(r * k2 * r_k, axis=1, keepdims=True) * v
    sgb[...] = _silu(g)

    def step(t, carry):
        def pass1(kq, sa):
            return sa + st_ref[kq] * ab[t, pl.ds(kq, 1), :]
        sa = lax.fori_loop(0, n, pass1, jnp.zeros((n, LANES), F32), unroll=8)
        vt = vb[t]

        def pass2(kq, y):
            s_new = (st_ref[kq] * wb[t, pl.ds(kq, 1), :] + sa * bb[t, pl.ds(kq, 1), :]
                     + vt * kb[t, pl.ds(kq, 1), :])
            st_ref[kq] = s_new
            return y + s_new * rb[t, pl.ds(kq, 1), :]
        yb[t] = lax.fori_loop(0, n, pass2, jnp.zeros((n, LANES), F32), unroll=8)
        return carry

    lax.fori_loop(0, c, step, 0)

    y = yb[...]
    mean = jnp.sum(y, axis=1, keepdims=True) * (1.0 / n)
    d = y - mean
    var = jnp.sum(d * d, axis=1, keepdims=True) * (1.0 / n)
    yn = d * lax.rsqrt(var + RWKV_GN_EPS) * ln_w + ln_b
    h = (yn + bonb[...]) * sgb[...]
    h_ref[0, 0] = h[:, :, :RWKV_HEADS].astype(h_ref.dtype)
    h_ref[0, 1] = h[:, :, RWKV_HEADS:].astype(h_ref.dtype)

    @pl.when(pl.program_id(1) == pl.num_programs(1) - 1)
    def _():
        s_out_ref[...] = st_ref[...]


def _rwkv_mixer(xn, shift, wkv, w_in, mu, w0, w_up, a0, a_up, k_k, k_a, r_k, ln_w, ln_b):
    b, t, d = xn.shape
    hh, n, e = RWKV_HEADS, RWKV_HEAD, D_INNER
    g2 = b // 2
    lw = 2 * RWKV_LORA
    to_nh = lambda x: x.reshape(x.shape[:-1] + (hh, n)).swapaxes(-1, -2)
    twice = lambda x: jnp.concatenate([x, x], axis=-1)
    w_main = to_nh(w_in[:, :4 * e].reshape(d, 4, e)).reshape(d, 4 * e)
    w_lora = w_in[:, 4 * e:]
    x2 = xn.reshape(b * t, d)
    pm = _matmul(x2, w_main).reshape(g2, 2, t, 4, n, hh)
    plo = _matmul(x2, w_lora).reshape(b, t, lw)
    if shift is None:
        p0m = jnp.zeros((g2, 4, n, 2 * hh), F32)
        p0l = jnp.zeros((b, 1, lw), F32)
    else:
        sb = shift.astype(BF16)
        p0m = _matmul(sb, w_main).reshape(g2, 2, 4, n, hh).transpose(0, 2, 3, 1, 4).reshape(g2, 4, n, 2 * hh)
        p0l = _matmul(sb, w_lora).reshape(b, 1, lw)
    wup = jnp.zeros((lw, e), F32).at[:RWKV_LORA].set(to_nh(w_up).reshape(RWKV_LORA, e))
    aup = jnp.zeros((lw, e), F32).at[RWKV_LORA:].set(to_nh(a_up).reshape(RWKV_LORA, e))
    w_lin, a_lin = _rwkv_lora(plo, p0l, mu[4 * e:].reshape(1, lw), wup, aup)
    w_lin = w_lin.reshape(g2, 2, t, n, hh)
    a_lin = a_lin.reshape(g2, 2, t, n, hh)
    mu4 = twice(to_nh(mu[:4 * e].reshape(4, e)))
    pvec = jnp.stack([twice(to_nh(x)) for x in (w0, a0, k_k, k_a)] + [twice(r_k.T)]
                     + [twice(to_nh(x)) for x in (ln_w, ln_b)])
    s0 = wkv.transpose(3, 2, 0, 1).reshape(n, n, b * hh)

    c = min(t, 16)
    tok6 = lambda q: pl.BlockSpec((1, 2, c, 1, n, hh), lambda gi, ci, q=q: (gi, 0, ci, q, 0, 0))
    tok5 = pl.BlockSpec((1, 2, c, n, hh), lambda gi, ci: (gi, 0, ci, 0, 0))
    st = pl.BlockSpec((n, n, LANES), lambda gi, ci: (0, 0, gi))
    chunk = pltpu.VMEM((c, n, LANES), F32)
    h, s1 = pl.pallas_call(
        functools.partial(_rwkv_core_kernel, c=c),
        grid=(g2, t // c),
        in_specs=[tok6(0), tok6(1), tok6(2), tok6(3), tok5, tok5,
                  pl.BlockSpec((1, 4, n, LANES), lambda gi, ci: (gi, 0, 0, 0)),
                  pl.BlockSpec((4, n, LANES), lambda gi, ci: (0, 0, 0)),
                  pl.BlockSpec((7, n, LANES), lambda gi, ci: (0, 0, 0)), st],
        out_specs=[tok5, st],
        out_shape=[jax.ShapeDtypeStruct((g2, 2, t, n, hh), BF16), jax.ShapeDtypeStruct((n, n, b * hh), F32)],
        scratch_shapes=[pltpu.VMEM((n, n, LANES), F32), pltpu.VMEM((4, n, LANES), F32)] + [chunk] * 9,
        compiler_params=_params(("parallel", "arbitrary")),
        name="rwkv_core",
    )(pm, pm, pm, pm, w_lin, a_lin, p0m, mu4, pvec, s0)
    wkv_new = s1.reshape(n, n, b, hh).transpose(2, 3, 1, 0)
    return h.reshape(b * t, e), wkv_new


def _ret_kernel(lg_ref, q_ref, k_ref, v_ref, g_ref, cos_ref, sin_ref, s0_ref, h_ref, s_out_ref, s_ref, *, c, hb):
    @pl.when(pl.program_id(2) == 0)
    def _():
        s_ref[...] = s0_ref[0]

    cos = cos_ref[...]
    sin = sin_ref[...]
    half = RET_DK // 2

    def rot(x):
        x1, x2 = x[:, :half], x[:, half:]
        return jnp.concatenate([x1 * cos - x2 * sin, x1 * sin + x2 * cos], axis=1)

    ti = lax.broadcasted_iota(jnp.int32, (c, c), 0)
    si = lax.broadcasted_iota(jnp.int32, (c, c), 1)
    causal = ti >= si
    dist = (ti - si).astype(F32)
    trow = lax.broadcasted_iota(jnp.int32, (c, RET_DK), 0).astype(F32)
    for j in range(hb):
        lg = lg_ref[pl.program_id(1) * hb + j]
        q = rot(q_ref[0, :, j * RET_DK:(j + 1) * RET_DK])
        k = rot(k_ref[0, :, j * RET_DK:(j + 1) * RET_DK]) * (RET_DK ** -0.5)
        v = v_ref[0, :, j * RET_DV:(j + 1) * RET_DV]
        dmat = jnp.exp(jnp.where(causal, dist * lg, -jnp.inf))
        s = s_ref[j]
        att = _bdot_nt(q, k) * dmat
        y = _bdot(att, v) + _bdot(q * jnp.exp((trow + 1.0) * lg), s)
        s_ref[j] = (jnp.exp(jnp.zeros((1, RET_DV), F32) + lg * c) * s
                    + _bdot_tn(k * jnp.exp((c - 1.0 - trow) * lg), v))
        yn = y * lax.rsqrt(jnp.mean(y * y, axis=-1, keepdims=True) + NORM_EPS)
        h_ref[0, :, j * RET_DV:(j + 1) * RET_DV] = (
            yn * _silu(g_ref[0, :, j * RET_DV:(j + 1) * RET_DV])).astype(h_ref.dtype)

    @pl.when(pl.program_id(2) == pl.num_programs(2) - 1)
    def _():
        s_out_ref[0] = s_ref[...]


def _ret_mixer(xn, pos, s0, w_in):
    b, t, d = xn.shape
    p = _matmul(xn.reshape(b * t, d), w_in).reshape(b, t, -1)
    c = min(t, 256)
    hb = 1 if t > 8 else RET_HEADS
    half = RET_DK // 2
    inv = 1.0 / (ROPE_BASE ** jnp.linspace(0.0, 1.0, half, dtype=F32))
    ang = pos.astype(F32)[:, None] * inv[None, :]
    cos, sin = jnp.cos(ang), jnp.sin(ang)
    log_gamma = jnp.log1p(-(2.0 ** (-5.0 - jnp.arange(RET_HEADS, dtype=F32))))
    nblk = RET_HEADS // hb
    spec = lambda w, off: pl.BlockSpec((1, c, hb * w), lambda bi, hi, ci, lg, off=off: (bi, ci, off + hi))
    tab = pl.BlockSpec((c, half), lambda bi, hi, ci, lg: (ci, 0))
    st = pl.BlockSpec((1, hb, RET_DK, RET_DV), lambda bi, hi, ci, lg: (bi, hi, 0, 0))
    h, s1 = pl.pallas_call(
        functools.partial(_ret_kernel, c=c, hb=hb),
        grid_spec=pltpu.PrefetchScalarGridSpec(
            num_scalar_prefetch=1,
            grid=(b, nblk, t // c),
            in_specs=[spec(RET_DK, 0), spec(RET_DK, nblk), spec(RET_DV, nblk), spec(RET_DV, 2 * nblk), tab, tab, st],
            out_specs=[pl.BlockSpec((1, c, hb * RET_DV), lambda bi, hi, ci, lg: (bi, ci, hi)), st],
            scratch_shapes=[pltpu.VMEM((hb, RET_DK, RET_DV), F32)],
        ),
        out_shape=[jax.ShapeDtypeStruct((b, t, D_INNER), BF16), jax.ShapeDtypeStruct(s0.shape, F32)],
        compiler_params=_params(("parallel", "parallel", "arbitrary")),
        name="retention",
    )(log_gamma, p, p, p, p, cos, sin, s0)
    return h.reshape(b * t, D_INNER), s1


def _causal_conv(buf_ref, x, w, bias, c):
    buf_ref[8:8 + c, :] = x
    acc = bias
    for j in range(SSD_CONV):
        acc = acc + w[j:j + 1, :] * buf_ref[5 + j:5 + j + c, :]
    buf_ref[5:8, :] = buf_ref[5 + c:8 + c, :]
    return acc


def _ssd_kernel(z_ref, x_ref, bm_ref, cm_ref, dt_ref, cx_ref, cb_ref, cc_ref,
                wx_ref, wb_ref, wc_ref, bx_ref, bb_ref, bc_ref, dtb_ref, alog_ref, dsk_ref, nw_ref, s0_ref,
                h_ref, s_out_ref, bufx_ref, bufb_ref, bufc_ref, s_ref, *, c, gb):
    @pl.when(pl.program_id(2) == 0)
    def _():
        s_ref[...] = s0_ref[0]
        bufx_ref[5:8, :] = cx_ref[0]
        bufb_ref[5:8, :] = cb_ref[0]
        bufc_ref[5:8, :] = cc_ref[0]

    xs_all = _silu(_causal_conv(bufx_ref, x_ref[0], wx_ref[...], bx_ref[...], c))
    bm_all = _silu(_causal_conv(bufb_ref, bm_ref[0], wb_ref[...], bb_ref[...], c))
    cm_all = _silu(_causal_conv(bufc_ref, cm_ref[0], wc_ref[...], bc_ref[...], c))

    lane = lax.broadcasted_iota(jnp.int32, (c, LANES), 1)
    dt = jnp.where(lane < SSD_HEADS, _softplus(dt_ref[0] + dtb_ref[...]), 0.0)
    la = dt * (-jnp.exp(alog_ref[...]))
    lcum = _cumsum_rows(la)

    ti = lax.broadcasted_iota(jnp.int32, (c, c), 0)
    si = lax.broadcasted_iota(jnp.int32, (c, c), 1)
    causal = ti >= si
    eye = ti == si
    last = lax.broadcasted_iota(jnp.int32, (c, 1), 0) == c - 1
    gw, ns = SSD_GW, SSD_STATE
    for gl in range(gb):
        xs = xs_all[:, gl * gw:(gl + 1) * gw]
        bm = bm_all[:, gl * ns:(gl + 1) * ns]
        cm = cm_all[:, gl * ns:(gl + 1) * ns]
        scores = _bdot_nt(cm, bm)
        ys = []
        for r in range(SSD_RPG):
            head = (pl.program_id(1) * gb + gl) * SSD_RPG + r
            sel = lane == head
            lcol = jnp.sum(jnp.where(sel, lcum, 0.0), axis=-1, keepdims=True)
            dtc = jnp.sum(jnp.where(sel, dt, 0.0), axis=-1, keepdims=True)
            lrow = jnp.sum(jnp.where(eye, lcol, 0.0), axis=0, keepdims=True)
            lend = jnp.sum(jnp.where(last, lcol, 0.0), axis=0, keepdims=True)
            dec = jnp.exp(jnp.where(causal, lcol - lrow, -jnp.inf))
            v_r = xs[:, r * SSD_HEADDIM:(r + 1) * SSD_HEADDIM] * dtc
            s_r = s_ref[gl * SSD_RPG + r]
            ys.append(_bdot(scores * dec, v_r) + jnp.exp(lcol) * _bdot(cm, s_r))
            s_ref[gl * SSD_RPG + r] = jnp.exp(lend) * s_r + _bdot_tn(bm, jnp.exp(lend - lcol) * v_r)
        sl = slice(gl * gw, (gl + 1) * gw)
        y = jnp.concatenate(ys, axis=1) + xs * dsk_ref[:, sl]
        y = y * _silu(z_ref[0, :, sl])
        yn = y * lax.rsqrt(jnp.mean(y * y, axis=-1, keepdims=True) + SSD_NORM_EPS)
        h_ref[0, :, sl] = (yn * nw_ref[:, sl]).astype(h_ref.dtype)

    @pl.when(pl.program_id(2) == pl.num_programs(2) - 1)
    def _():
        s_out_ref[0] = s_ref[...]


def _ssd_mixer(xn, conv_state, s0, w_in, conv_w, conv_b, dt_bias, a_log, d_skip, norm_w):
    b, t, d = xn.shape
    p = _matmul(xn.reshape(b * t, d), w_in).reshape(b, t, -1)
    c = min(t, 64)
    gb = 1 if t > 8 else SSD_GROUPS
    e, ng = D_INNER, SSD_GROUPS
    gw, st = gb * SSD_GW, gb * SSD_STATE
    nblk = ng // gb
    xoff = e // gw
    boff = (2 * e) // st
    coff = boff + nblk
    dtoff = (2 * e + 2 * ng * SSD_STATE) // LANES
    tokw = lambda w, off: pl.BlockSpec((1, c, w), lambda bi, gi, ci, off=off: (bi, ci, off + gi))
    cs = lambda w, off: pl.BlockSpec((1, SSD_CONV - 1, w), lambda bi, gi, ci, off=off: (bi, 0, off + gi))
    cw = lambda w, off: pl.BlockSpec((SSD_CONV, w), lambda bi, gi, ci, off=off: (0, off + gi))
    cbv = lambda w, off: pl.BlockSpec((1, w), lambda bi, gi, ci, off=off: (0, off + gi))
    hv = pl.BlockSpec((1, LANES), lambda bi, gi, ci: (0, 0))
    pad_heads = lambda x: jnp.zeros((1, LANES), F32).at[0, :SSD_HEADS].set(x)
    cb2 = conv_b.reshape(1, -1)
    nh = gb * SSD_RPG
    stspec = pl.BlockSpec((1, nh, SSD_STATE, SSD_HEADDIM), lambda bi, gi, ci: (bi, gi, 0, 0))
    h, s1 = pl.pallas_call(
        functools.partial(_ssd_kernel, c=c, gb=gb),
        grid=(b, nblk, t // c),
        in_specs=[tokw(gw, 0), tokw(gw, xoff), tokw(st, boff), tokw(st, coff),
                  pl.BlockSpec((1, c, LANES), lambda bi, gi, ci: (bi, ci, dtoff)),
                  cs(gw, 0), cs(st, e // st), cs(st, e // st + nblk),
                  cw(gw, 0), cw(st, e // st), cw(st, e // st + nblk),
                  cbv(gw, 0), cbv(st, e // st), cbv(st, e // st + nblk),
                  hv, hv, cbv(gw, 0), cbv(gw, 0), stspec],
        out_specs=[tokw(gw, 0), stspec],
        out_shape=[jax.ShapeDtypeStruct((b, t, e), BF16), jax.ShapeDtypeStruct(s0.shape, F32)],
        scratch_shapes=[pltpu.VMEM((8 + c, gw), F32), pltpu.VMEM((8 + c, st), F32), pltpu.VMEM((8 + c, st), F32),
                        pltpu.VMEM((nh, SSD_STATE, SSD_HEADDIM), F32)],
        compiler_params=_params(("parallel", "parallel", "arbitrary")),
        name="ssd",
    )(p, p, p, p, p, conv_state, conv_state, conv_state, conv_w, conv_w, conv_w, cb2, cb2, cb2,
      pad_heads(dt_bias), pad_heads(a_log), jnp.repeat(d_skip, SSD_HEADDIM).reshape(1, e),
      norm_w.reshape(1, e), s0)
    conv_new = p[:, t - (SSD_CONV - 1):, e:e + e + 2 * ng * SSD_STATE]
    return h.reshape(b * t, e), conv_new, s1


def _gla_kernel(*refs, c, hb):
    q_refs, k_refs, v_refs, g_refs = (refs[i * hb:(i + 1) * hb] for i in range(4))
    ad_ref, aup_ref, ab_ref, nw_ref, s0_ref, h_ref, s_out_ref, s_ref = refs[4 * hb:]

    @pl.when(pl.program_id(2) == 0)
    def _():
        s_ref[...] = s0_ref[0]

    lane = lax.broadcasted_iota(jnp.int32, (c, LANES), 1)
    ad = jnp.where(lane < GLA_LORA, ad_ref[0], 0.0)
    la_all = -_softplus(-(_bdot(ad, aup_ref[...]) + ab_ref[...])) / GLA_LOGIT_NORM
    ti = lax.broadcasted_iota(jnp.int32, (c, c), 0)
    si = lax.broadcasted_iota(jnp.int32, (c, c), 1)
    rowi = lax.broadcasted_iota(jnp.int32, (c, GLA_DK), 0)
    eye = (lax.broadcasted_iota(jnp.int32, (GLA_DK, GLA_DK), 0)
           == lax.broadcasted_iota(jnp.int32, (GLA_DK, GLA_DK), 1))
    for j in range(hb):
        lcum = _cumsum_rows(la_all[:, j * GLA_DK:(j + 1) * GLA_DK])
        q = q_refs[j][0] * (GLA_DK ** -0.5)
        k = k_refs[j][0]
        v = v_refs[j][0]
        qe = q * jnp.exp(lcum)
        ke = k * jnp.exp(-lcum)
        att = jnp.where(ti >= si, _bdot_nt(qe, ke), 0.0)
        lend = jnp.sum(jnp.where(rowi == c - 1, lcum, 0.0), axis=0, keepdims=True)
        kend = k * jnp.exp(lend - lcum)
        s = s_ref[j]
        y = _bdot(att, v) + _bdot(qe, s)
        gend_col = jnp.sum(jnp.where(eye, jnp.exp(lend), 0.0), axis=1, keepdims=True)
        s_ref[j] = gend_col * s + _bdot_tn(kend, v)
        yn = y * lax.rsqrt(jnp.mean(y * y, axis=-1, keepdims=True) + NORM_EPS) * nw_ref[...]
        h_ref[0, :, j * GLA_DV:(j + 1) * GLA_DV] = (yn * _silu(g_refs[j][0])).astype(h_ref.dtype)

    @pl.when(pl.program_id(2) == pl.num_programs(2) - 1)
    def _():
        s_out_ref[0] = s_ref[...]


def _gla_mixer(xn, s0, w_in, a_up, a_bias, norm_w):
    b, t, d = xn.shape
    p = _matmul(xn.reshape(b * t, d), w_in).reshape(b, t, -1)
    c = min(t, 16)
    hb = GLA_HEADS
    dk, dv = GLA_DK, GLA_DV
    nblk = GLA_HEADS // hb
    tokw = lambda w, off, j: pl.BlockSpec((1, c, w), lambda bi, hi, ci, off=off, j=j: (bi, ci, off + hi * hb + j))
    heads = lambda w, off: [tokw(w, off, j) for j in range(hb)]
    aup = jnp.zeros((LANES, GLA_KEY), F32).at[:GLA_LORA].set(a_up)
    st = pl.BlockSpec((1, hb, dk, dv), lambda bi, hi, ci: (bi, hi, 0, 0))
    h, s1 = pl.pallas_call(
        functools.partial(_gla_kernel, c=c, hb=hb),
        grid=(b, nblk, t // c),
        in_specs=heads(dk, 0) + heads(dk, GLA_HEADS) + heads(dv, (2 * GLA_KEY) // dv)
        + heads(dv, (2 * GLA_KEY + D_INNER) // dv)
        + [pl.BlockSpec((1, c, LANES), lambda bi, hi, ci: (bi, ci, (2 * GLA_KEY + 2 * D_INNER) // LANES)),
           pl.BlockSpec((LANES, hb * dk), lambda bi, hi, ci: (0, hi)),
           pl.BlockSpec((1, hb * dk), lambda bi, hi, ci: (0, hi)),
           pl.BlockSpec((1, dv), lambda bi, hi, ci: (0, 0)), st],
        out_specs=[pl.BlockSpec((1, c, hb * dv), lambda bi, hi, ci: (bi, ci, hi)), st],
        out_shape=[jax.ShapeDtypeStruct((b, t, D_INNER), BF16), jax.ShapeDtypeStruct(s0.shape, F32)],
        scratch_shapes=[pltpu.VMEM((hb, dk, dv), F32)],
        compiler_params=_params(("parallel", "parallel", "arbitrary")),
        name="gla",
    )(*([p] * (4 * hb + 1)), aup, a_bias.reshape(1, -1), norm_w.reshape(1, -1), s0)
    return h.reshape(b * t, D_INNER), s1


def _trunk(x, pos, states, norm_gains, final_norm, rwkv_p, ret_p, ssd_p, gla_p):
    shift, wkv, ret_s, conv_s, ssd_s, gla_s = states
    b, t, d = x.shape
    x2 = x.reshape(b * t, d)

    def normed(i):
        return _rmsnorm(x2, norm_gains[i], BF16).reshape(b, t, d)

    (w_in, mu, w0, w_up, a0, a_up, k_k, k_a, r_k, ln_w, ln_b, w_out) = rwkv_p
    shift_new = _rmsnorm(x[:, -1], norm_gains[0], F32)
    h, wkv = _rwkv_mixer(normed(0), shift, wkv, w_in, mu, w0, w_up, a0, a_up, k_k, k_a, r_k, ln_w, ln_b)
    w_out_nh = w_out.reshape(RWKV_HEADS, RWKV_HEAD, d).swapaxes(0, 1).reshape(D_INNER, d)
    x2 = _matmul(h, w_out_nh, res=x2, tn=512)
    w_in, w_out = ret_p
    h, ret_s = _ret_mixer(normed(1), pos, ret_s, w_in)
    x2 = _matmul(h, w_out, res=x2, tn=512)
    (w_in, conv_w, conv_b, dt_bias, a_log, d_skip, norm_w, w_out) = ssd_p
    h, conv_s, ssd_s = _ssd_mixer(normed(2), conv_s, ssd_s, w_in, conv_w, conv_b, dt_bias, a_log, d_skip, norm_w)
    x2 = _matmul(h, w_out, res=x2, tn=512)
    (w_in, a_up, a_bias, norm_w, w_out) = gla_p
    h, gla_s = _gla_mixer(normed(3), gla_s, w_in, a_up, a_bias, norm_w)
    x2 = _matmul(h, w_out, res=x2, tn=512)

    y = _rmsnorm(x2, final_norm, F32).reshape(b, t, d)
    return y, (shift_new, wkv, ret_s, conv_s, ssd_s, gla_s)


def kernel(x_prompt, x_sample, state_rwkv_shift, state_rwkv_wkv, state_ret, state_ssd_conv, state_ssd, state_gla, norm_gains, final_norm, rwkv_w_in, rwkv_mu, rwkv_w0, rwkv_w_up, rwkv_a0, rwkv_a_up, rwkv_k_k, rwkv_k_a, rwkv_r_k, rwkv_ln_w, rwkv_ln_b, rwkv_w_out, ret_w_in, ret_w_out, ssd_w_in, ssd_conv_w, ssd_conv_b, ssd_dt_bias, ssd_A_log, ssd_D, ssd_norm_w, ssd_w_out, gla_w_in, gla_a_up, gla_a_bias, gla_norm_w, gla_w_out):
    rwkv_p = (rwkv_w_in, rwkv_mu, rwkv_w0, rwkv_w_up, rwkv_a0, rwkv_a_up, rwkv_k_k, rwkv_k_a, rwkv_r_k,
              rwkv_ln_w, rwkv_ln_b, rwkv_w_out)
    ret_p = (ret_w_in, ret_w_out)
    ssd_p = (ssd_w_in, ssd_conv_w, ssd_conv_b, ssd_dt_bias, ssd_A_log, ssd_D, ssd_norm_w, ssd_w_out)
    gla_p = (gla_w_in, gla_a_up, gla_a_bias, gla_norm_w, gla_w_out)

    bp, tp = x_prompt.shape[0], x_prompt.shape[1]
    zeros = lambda ref: jnp.zeros((bp,) + ref.shape[1:], F32)
    init_prompt = (None, zeros(state_rwkv_wkv), zeros(state_ret), zeros(state_ssd_conv),
                   zeros(state_ssd), zeros(state_gla))
    y_p, st_p = _trunk(x_prompt, jnp.arange(tp), init_prompt, norm_gains, final_norm, rwkv_p, ret_p, ssd_p, gla_p)
    init_sample = (state_rwkv_shift, state_rwkv_wkv, state_ret, state_ssd_conv, state_ssd, state_gla)
    y_s, st_s = _trunk(x_sample, PAST_LEN + jnp.arange(x_sample.shape[1]), init_sample, norm_gains, final_norm,
                       rwkv_p, ret_p, ssd_p, gla_p)
    return (y_p, y_s) + tuple(st_p) + tuple(st_s)
```

```python
import functools

import jax
import jax.numpy as jnp
from jax import lax
from jax.experimental import pallas as pl
from jax.experimental.pallas import tpu as pltpu

F32 = jnp.float32
BF16 = jnp.bfloat16

D_MODEL = 2048
D_INNER = 2 * D_MODEL
NORM_EPS = 1e-6
PAST_LEN = 16384

RWKV_HEAD = 64
RWKV_HEADS = D_INNER // RWKV_HEAD
RWKV_LORA = 96
RWKV_GN_EPS = 64e-5

RET_HEADS = 8
RET_DK = D_MODEL // RET_HEADS
RET_DV = D_INNER // RET_HEADS
ROPE_BASE = 10000.0

SSD_HEADDIM = 64
SSD_HEADS = D_INNER // SSD_HEADDIM
SSD_STATE = 128
SSD_GROUPS = 8
SSD_RPG = SSD_HEADS // SSD_GROUPS
SSD_CONV = 4
SSD_NORM_EPS = 1e-5
SSD_GW = D_INNER // SSD_GROUPS

GLA_HEADS = 4
GLA_KEY = D_MODEL // 2
GLA_DK = GLA_KEY // GLA_HEADS
GLA_DV = D_INNER // GLA_HEADS
GLA_LORA = 16
GLA_LOGIT_NORM = 16.0

LANES = 128
VMEM_LIMIT = 52 * 1024 * 1024


def _params(sem):
    return pltpu.CompilerParams(dimension_semantics=sem, vmem_limit_bytes=VMEM_LIMIT)


def _sigmoid(x):
    return 1.0 / (1.0 + jnp.exp(-x))


def _silu(x):
    return x * _sigmoid(x)


def _softplus(x):
    return jnp.maximum(x, 0.0) + jnp.log1p(jnp.exp(-jnp.abs(x)))


def _bdot(a, b):
    return jnp.dot(a.astype(BF16), b.astype(BF16), preferred_element_type=F32)


def _bdot_nt(a, b):
    return lax.dot_general(a.astype(BF16), b.astype(BF16), (((1,), (1,)), ((), ())),
                           preferred_element_type=F32)


def _bdot_tn(a, b):
    return lax.dot_general(a.astype(BF16), b.astype(BF16), (((0,), (0,)), ((), ())),
                           preferred_element_type=F32)


def _cumsum_rows(x):
    c = x.shape[0]
    row = lax.broadcasted_iota(jnp.int32, x.shape, 0)
    if c % 8 == 0:
        s = 1
        while s < c:
            x = x + jnp.where(row >= s, pltpu.roll(x, s, axis=0), 0.0)
            s *= 2
        return x
    acc = jnp.zeros_like(x)
    for j in range(c):
        acc = acc + jnp.where(row >= j, x[j:j + 1, :], 0.0)
    return acc


def _norm_kernel(x_ref, g_ref, o_ref):
    x = x_ref[...]
    ms = jnp.mean(x * x, axis=-1, keepdims=True)
    o_ref[...] = (x * lax.rsqrt(ms + NORM_EPS) * g_ref[...]).astype(o_ref.dtype)


def _rmsnorm(x2d, gain, out_dtype):
    m, d = x2d.shape
    tm = 512 if m % 512 == 0 else m
    return pl.pallas_call(
        _norm_kernel,
        grid=(m // tm,),
        in_specs=[pl.BlockSpec((tm, d), lambda i: (i, 0)), pl.BlockSpec((1, d), lambda i: (0, 0))],
        out_specs=pl.BlockSpec((tm, d), lambda i: (i, 0)),
        out_shape=jax.ShapeDtypeStruct((m, d), out_dtype),
        compiler_params=_params(("parallel",)),
        name="rmsnorm",
    )(x2d, gain.reshape(1, d))


def _mm_kernel(a_ref, w_ref, *rest, has_res):
    if has_res:
        r_ref, o_ref, wb_ref = rest
    else:
        o_ref, wb_ref = rest

    @pl.when(pl.program_id(1) == 0)
    def _():
        wb_ref[...] = w_ref[...].astype(BF16)

    acc = jnp.dot(a_ref[...], wb_ref[...], preferred_element_type=F32)
    if has_res:
        acc = r_ref[...] + acc
    o_ref[...] = acc.astype(o_ref.dtype)


def _matmul(a, w, res=None, tn=1024):
    m, k = a.shape
    n = w.shape[1]
    tm = next((c for c in (1024, 512) if m % c == 0), m)
    tn = min(tn, n)
    grid = (pl.cdiv(n, tn), m // tm)
    in_specs = [pl.BlockSpec((tm, k), lambda j, i: (i, 0)), pl.BlockSpec((k, tn), lambda j, i: (0, j))]
    args = [a, w]
    if res is not None:
        in_specs.append(pl.BlockSpec((tm, tn), lambda j, i: (i, j)))
        args.append(res)
    return pl.pallas_call(
        functools.partial(_mm_kernel, has_res=res is not None),
        grid=grid,
        in_specs=in_specs,
        out_specs=pl.BlockSpec((tm, tn), lambda j, i: (i, j)),
        out_shape=jax.ShapeDtypeStruct((m, n), F32),
        scratch_shapes=[pltpu.VMEM((k, tn), BF16)],
        compiler_params=_params(("arbitrary", "arbitrary")),
        name="matmul",
    )(*args)


def _shifted(buf_ref, carry_ref, x, c):
    buf_ref[7:8, :] = carry_ref[...]
    buf_ref[8:8 + c, :] = x
    prev = buf_ref[7:7 + c, :]
    carry_ref[...] = buf_ref[8 + c - 1:8 + c, :]
    return prev


def _head_sum(x):
    s = x[:, :LANES]
    for m in range(1, x.shape[1] // LANES):
        s = s + x[:, m * LANES:(m + 1) * LANES]
    return s + pltpu.roll(s, RWKV_HEADS, axis=1)


def _rwkv_kernel(pr_ref, pk_ref, pv_ref, pg_ref, pl_ref, sr_ref, sk_ref, sv_ref, sg_ref, sl_ref,
                 mr_ref, mk_ref, mv_ref, mg_ref, ml_ref, pvec_ref, lnw_ref, lnb_ref, wup_ref, aup_ref, s0_ref,
                 h_ref, s_out_ref,
                 st_ref, buf_ref, bufl_ref, cr_ref, ck_ref, cv_ref, cg_ref, cl_ref,
                 rb, wb, kb, ab, bb, vb, vt_ref, yt_ref, *, c):
    n = RWKV_HEAD
    nk = n // 2

    @pl.when(pl.program_id(1) == 0)
    def _():
        st_ref[...] = s0_ref[0]
        cr_ref[...] = sr_ref[0]
        ck_ref[...] = sk_ref[0]
        cv_ref[...] = sv_ref[0]
        cg_ref[...] = sg_ref[0]
        cl_ref[...] = sl_ref[0]

    def lerp(p_ref, carry_ref, mu_ref, buf):
        p = p_ref[0]
        return p + mu_ref[...] * (_shifted(buf, carry_ref, p, c) - p)

    r = lerp(pr_ref, cr_ref, mr_ref, buf_ref)
    k = lerp(pk_ref, ck_ref, mk_ref, buf_ref)
    v = lerp(pv_ref, cv_ref, mv_ref, buf_ref)
    g = lerp(pg_ref, cg_ref, mg_ref, buf_ref)
    lo = lerp(pl_ref, cl_ref, ml_ref, bufl_ref)
    lane_lo = lax.broadcasted_iota(jnp.int32, lo.shape, 1)
    lo = jnp.where(lane_lo < 2 * RWKV_LORA, lo, 0.0)

    w0, a0, k_k, k_a, r_k = (pvec_ref[i:i + 1, :] for i in range(5))
    decay = jnp.exp(-jnp.exp(-_softplus(-(w0 + _bdot(jnp.tanh(lo), wup_ref[...]))) - 0.5))
    a = _sigmoid(a0 + _bdot(lo, aup_ref[...]))
    kk32 = k * k_k
    nrm = jnp.sqrt(_head_sum(kk32 * kk32))
    kk = kk32 / jnp.tile(jnp.maximum(nrm, 1e-12), (1, nk))
    k2 = k * (1.0 + (a - 1.0) * k_a)
    coef = _head_sum(r * k2 * r_k)
    b = kk * a
    low_half = lax.broadcasted_iota(jnp.int32, (c, LANES), 1) < RWKV_HEADS

    def dup_rows(dst, x):
        for m in range(nk):
            slab = x[:, m * LANES:(m + 1) * LANES]
            swapped = pltpu.roll(slab, RWKV_HEADS, axis=1)
            dst[2 * m] = jnp.where(low_half, slab, swapped)
            dst[2 * m + 1] = jnp.where(low_half, swapped, slab)

    dup_rows(rb, r)
    dup_rows(wb, decay)
    dup_rows(kb, k2)
    dup_rows(ab, -kk)
    dup_rows(bb, b)
    for m in range(nk):
        vb[m] = v[:, m * LANES:(m + 1) * LANES]
    vt_ref[...] = jnp.swapaxes(vb[...], 0, 1)

    def step(t, carry):
        def pass1(kq, sa):
            return sa + st_ref[kq] * ab[kq, pl.ds(t, 1), :]
        sa = lax.fori_loop(0, n, pass1, jnp.zeros((nk, LANES), F32), unroll=16)
        vt = vt_ref[t]

        def pass2(kq, y):
            s_new = (st_ref[kq] * wb[kq, pl.ds(t, 1), :] + sa * bb[kq, pl.ds(t, 1), :]
                     + vt * kb[kq, pl.ds(t, 1), :])
            st_ref[kq] = s_new
            return y + s_new * rb[kq, pl.ds(t, 1), :]
        yt_ref[t] = lax.fori_loop(0, n, pass2, jnp.zeros((nk, LANES), F32), unroll=16)
        return carry

    lax.fori_loop(0, c, step, 0)

    def over_values(x):
        s = jnp.sum(x, axis=0)
        return (s + pltpu.roll(s, RWKV_HEADS, axis=1)) * (1.0 / n)

    y3 = jnp.swapaxes(yt_ref[...], 0, 1)
    d = y3 - over_values(y3)[None]
    var = over_values(d * d)
    hv = d * lax.rsqrt(var + RWKV_GN_EPS)[None] * lnw_ref[...] + lnb_ref[...] + coef[None] * vb[...]
    sg = _silu(g)
    for m in range(nk):
        sl = slice(m * LANES, (m + 1) * LANES)
        h_ref[0, :, sl] = (hv[m] * sg[:, sl]).astype(h_ref.dtype)

    @pl.when(pl.program_id(1) == pl.num_programs(1) - 1)
    def _():
        s_out_ref[0] = st_ref[...]


def _rwkv_mixer(xn, shift, wkv, w_in, mu, w0, w_up, a0, a_up, k_k, k_a, r_k, ln_w, ln_b):
    b, t, d = xn.shape
    hh, n, e = RWKV_HEADS, RWKV_HEAD, D_INNER
    nk = n // 2
    lw = 2 * LANES
    lblk = (4 * e) // lw
    to_nh = lambda x: x.reshape(x.shape[:-1] + (hh, n)).swapaxes(-1, -2)
    flat = lambda x: x.reshape(x.shape[:-2] + (e,))
    w_perm = jnp.concatenate([flat(to_nh(w_in[:, :4 * e].reshape(d, 4, e))).reshape(d, 4 * e), w_in[:, 4 * e:]], axis=1)
    mu_perm = jnp.concatenate([flat(to_nh(mu[:4 * e].reshape(4, e))).reshape(4 * e), mu[4 * e:]]).reshape(1, -1)
    p = _matmul(xn.reshape(b * t, d), w_perm).reshape(b, t, -1)
    if shift is None:
        p0 = jnp.zeros((b, 1, p.shape[-1]), F32)
    else:
        p0 = _matmul(shift.astype(BF16), w_perm).reshape(b, 1, -1)
    wup = jnp.zeros((lw, e), F32).at[:RWKV_LORA].set(flat(to_nh(w_up))).astype(BF16)
    aup = jnp.zeros((lw, e), F32).at[RWKV_LORA:2 * RWKV_LORA].set(flat(to_nh(a_up))).astype(BF16)
    pvec = jnp.stack([flat(to_nh(x)) for x in (w0, a0, k_k, k_a)] + [r_k.T.reshape(e)])
    slabs = lambda x: flat(to_nh(x)).reshape(nk, 1, LANES)
    lnw3, lnb3 = slabs(ln_w), slabs(ln_b)
    s0 = wkv.reshape(b, hh, nk, 2, n).transpose(0, 4, 2, 3, 1).reshape(b, n, nk, LANES)

    c = min(t, 32)
    big = lambda j: pl.BlockSpec((1, c, e), lambda bi, ci, j=j: (bi, ci, j))
    big0 = lambda j: pl.BlockSpec((1, 1, e), lambda bi, ci, j=j: (bi, 0, j))
    mub = lambda j: pl.BlockSpec((1, e), lambda bi, ci, j=j: (0, j))
    const = lambda shp: pl.BlockSpec(shp, lambda bi, ci: (0,) * len(shp))
    st = pl.BlockSpec((1, n, nk, LANES), lambda bi, ci: (bi, 0, 0, 0))
    rows = pltpu.VMEM((n, c, LANES), F32)
    vals = pltpu.VMEM((nk, c, LANES), F32)
    tiles = pltpu.VMEM((c, nk, LANES), F32)
    h, s1 = pl.pallas_call(
        functools.partial(_rwkv_kernel, c=c),
        grid=(b, t // c),
        in_specs=([big(j) for j in range(4)] + [pl.BlockSpec((1, c, lw), lambda bi, ci: (bi, ci, lblk))]
                  + [big0(j) for j in range(4)] + [pl.BlockSpec((1, 1, lw), lambda bi, ci: (bi, 0, lblk))]
                  + [mub(j) for j in range(4)] + [pl.BlockSpec((1, lw), lambda bi, ci: (0, lblk))]
                  + [const((5, e)), const((nk, 1, LANES)), const((nk, 1, LANES)), const((lw, e)), const((lw, e)), st]),
        out_specs=[pl.BlockSpec((1, c, e), lambda bi, ci: (bi, ci, 0)), st],
        out_shape=[jax.ShapeDtypeStruct((b, t, e), BF16), jax.ShapeDtypeStruct((b, n, nk, LANES), F32)],
        scratch_shapes=[pltpu.VMEM((n, nk, LANES), F32), pltpu.VMEM((8 + c, e), F32), pltpu.VMEM((8 + c, lw), F32)]
        + [pltpu.VMEM((1, e), F32)] * 4 + [pltpu.VMEM((1, lw), F32)] + [rows] * 5 + [vals] + [tiles] * 2,
        compiler_params=_params(("parallel", "arbitrary")),
        name="rwkv",
    )(p, p, p, p, p, p0, p0, p0, p0, p0, mu_perm, mu_perm, mu_perm, mu_perm, mu_perm,
      pvec, lnw3, lnb3, wup, aup, s0)
    wkv_new = s1.reshape(b, n, nk, 2, hh).transpose(0, 4, 2, 3, 1).reshape(b, hh, n, n)
    return h.reshape(b * t, e), wkv_new


def _ret_kernel(lg_ref, q_ref, k_ref, v_ref, g_ref, cos_ref, sin_ref, s0_ref, h_ref, s_out_ref, s_ref, *, c, hb):
    @pl.when(pl.program_id(2) == 0)
    def _():
        s_ref[...] = s0_ref[0]

    cos = cos_ref[...]
    sin = sin_ref[...]
    half = RET_DK // 2

    def rot(x):
        x1, x2 = x[:, :half], x[:, half:]
        return jnp.concatenate([x1 * cos - x2 * sin, x1 * sin + x2 * cos], axis=1)

    ti = lax.broadcasted_iota(jnp.int32, (c, c), 0)
    si = lax.broadcasted_iota(jnp.int32, (c, c), 1)
    causal = ti >= si
    dist = (ti - si).astype(F32)
    trow = lax.broadcasted_iota(jnp.int32, (c, RET_DK), 0).astype(F32)
    for j in range(hb):
        lg = lg_ref[pl.program_id(1) * hb + j]
        q = rot(q_ref[0, :, j * RET_DK:(j + 1) * RET_DK])
        k = rot(k_ref[0, :, j * RET_DK:(j + 1) * RET_DK]) * (RET_DK ** -0.5)
        v = v_ref[0, :, j * RET_DV:(j + 1) * RET_DV]
        dmat = jnp.exp(jnp.where(causal, dist * lg, -jnp.inf))
        s = s_ref[j]
        att = _bdot_nt(q, k) * dmat
        y = _bdot(att, v) + _bdot(q * jnp.exp((trow + 1.0) * lg), s)
        s_ref[j] = (jnp.exp(jnp.zeros((1, RET_DV), F32) + lg * c) * s
                    + _bdot_tn(k * jnp.exp((c - 1.0 - trow) * lg), v))
        yn = y * lax.rsqrt(jnp.mean(y * y, axis=-1, keepdims=True) + NORM_EPS)
        h_ref[0, :, j * RET_DV:(j + 1) * RET_DV] = (
            yn * _silu(g_ref[0, :, j * RET_DV:(j + 1) * RET_DV])).astype(h_ref.dtype)

    @pl.when(pl.program_id(2) == pl.num_programs(2) - 1)
    def _():
        s_out_ref[0] = s_ref[...]


def _ret_mixer(xn, pos, s0, w_in):
    b, t, d = xn.shape
    p = _matmul(xn.reshape(b * t, d), w_in).reshape(b, t, -1)
    c = min(t, 256)
    hb = 1 if t > 8 else RET_HEADS
    half = RET_DK // 2
    inv = 1.0 / (ROPE_BASE ** jnp.linspace(0.0, 1.0, half, dtype=F32))
    ang = pos.astype(F32)[:, None] * inv[None, :]
    cos, sin = jnp.cos(ang), jnp.sin(ang)
    log_gamma = jnp.log1p(-(2.0 ** (-5.0 - jnp.arange(RET_HEADS, dtype=F32))))
    nblk = RET_HEADS // hb
    spec = lambda w, off: pl.BlockSpec((1, c, hb * w), lambda bi, hi, ci, lg, off=off: (bi, ci, off + hi))
    tab = pl.BlockSpec((c, half), lambda bi, hi, ci, lg: (ci, 0))
    st = pl.BlockSpec((1, hb, RET_DK, RET_DV), lambda bi, hi, ci, lg: (bi, hi, 0, 0))
    h, s1 = pl.pallas_call(
        functools.partial(_ret_kernel, c=c, hb=hb),
        grid_spec=pltpu.PrefetchScalarGridSpec(
            num_scalar_prefetch=1,
            grid=(b, nblk, t // c),
            in_specs=[spec(RET_DK, 0), spec(RET_DK, nblk), spec(RET_DV, nblk), spec(RET_DV, 2 * nblk), tab, tab, st],
            out_specs=[pl.BlockSpec((1, c, hb * RET_DV), lambda bi, hi, ci, lg: (bi, ci, hi)), st],
            scratch_shapes=[pltpu.VMEM((hb, RET_DK, RET_DV), F32)],
        ),
        out_shape=[jax.ShapeDtypeStruct((b, t, D_INNER), BF16), jax.ShapeDtypeStruct(s0.shape, F32)],
        compiler_params=_params(("parallel", "parallel", "arbitrary")),
        name="retention",
    )(log_gamma, p, p, p, p, cos, sin, s0)
    return h.reshape(b * t, D_INNER), s1


def _causal_conv(buf_ref, x, w, bias, c):
    buf_ref[8:8 + c, :] = x
    acc = bias
    for j in range(SSD_CONV):
        acc = acc + w[j:j + 1, :] * buf_ref[5 + j:5 + j + c, :]
    buf_ref[5:8, :] = buf_ref[5 + c:8 + c, :]
    return acc


def _ssd_kernel(z_ref, x_ref, bm_ref, cm_ref, dt_ref, cx_ref, cb_ref, cc_ref,
                wx_ref, wb_ref, wc_ref, bx_ref, bb_ref, bc_ref, dtb_ref, alog_ref, dsk_ref, nw_ref, s0_ref,
                h_ref, s_out_ref, bufx_ref, bufb_ref, bufc_ref, s_ref, *, c, gb):
    @pl.when(pl.program_id(2) == 0)
    def _():
        s_ref[...] = s0_ref[0]
        bufx_ref[5:8, :] = cx_ref[0]
        bufb_ref[5:8, :] = cb_ref[0]
        bufc_ref[5:8, :] = cc_ref[0]

    xs_all = _silu(_causal_conv(bufx_ref, x_ref[0], wx_ref[...], bx_ref[...], c))
    bm_all = _silu(_causal_conv(bufb_ref, bm_ref[0], wb_ref[...], bb_ref[...], c))
    cm_all = _silu(_causal_conv(bufc_ref, cm_ref[0], wc_ref[...], bc_ref[...], c))

    lane = lax.broadcasted_iota(jnp.int32, (c, LANES), 1)
    dt = jnp.where(lane < SSD_HEADS, _softplus(dt_ref[0] + dtb_ref[...]), 0.0)
    la = dt * (-jnp.exp(alog_ref[...]))
    lcum = _cumsum_rows(la)

    ti = lax.broadcasted_iota(jnp.int32, (c, c), 0)
    si = lax.broadcasted_iota(jnp.int32, (c, c), 1)
    causal = ti >= si
    eye = ti == si
    last = lax.broadcasted_iota(jnp.int32, (c, 1), 0) == c - 1
    gw, ns = SSD_GW, SSD_STATE
    for gl in range(gb):
        xs = xs_all[:, gl * gw:(gl + 1) * gw]
        bm = bm_all[:, gl * ns:(gl + 1) * ns]
        cm = cm_all[:, gl * ns:(gl + 1) * ns]
        scores = _bdot_nt(cm, bm)
        ys = []
        for r in range(SSD_RPG):
            head = (pl.program_id(1) * gb + gl) * SSD_RPG + r
            sel = lane == head
            lcol = jnp.sum(jnp.where(sel, lcum, 0.0), axis=-1, keepdims=True)
            dtc = jnp.sum(jnp.where(sel, dt, 0.0), axis=-1, keepdims=True)
            lrow = jnp.sum(jnp.where(eye, lcol, 0.0), axis=0, keepdims=True)
            lend = jnp.sum(jnp.where(last, lcol, 0.0), axis=0, keepdims=True)
            dec = jnp.exp(jnp.where(causal, lcol - lrow, -jnp.inf))
            v_r = xs[:, r * SSD_HEADDIM:(r + 1) * SSD_HEADDIM] * dtc
            s_r = s_ref[gl * SSD_RPG + r]
            ys.append(_bdot(scores * dec, v_r) + jnp.exp(lcol) * _bdot_nt(cm, s_r))
            s_ref[gl * SSD_RPG + r] = jnp.exp(lend) * s_r + _bdot_tn(jnp.exp(lend - lcol) * v_r, bm)
        sl = slice(gl * gw, (gl + 1) * gw)
        y = jnp.concatenate(ys, axis=1) + xs * dsk_ref[:, sl]
        y = y * _silu(z_ref[0, :, sl])
        yn = y * lax.rsqrt(jnp.mean(y * y, axis=-1, keepdims=True) + SSD_NORM_EPS)
        h_ref[0, :, sl] = (yn * nw_ref[:, sl]).astype(h_ref.dtype)

    @pl.when(pl.program_id(2) == pl.num_programs(2) - 1)
    def _():
        s_out_ref[0] = s_ref[...]


def _ssd_mixer(xn, conv_state, s0, w_in, conv_w, conv_b, dt_bias, a_log, d_skip, norm_w):
    b, t, d = xn.shape
    p = _matmul(xn.reshape(b * t, d), w_in).reshape(b, t, -1)
    c = min(t, 64)
    gb = 1 if t > 8 else SSD_GROUPS
    e, ng = D_INNER, SSD_GROUPS
    gw, st = gb * SSD_GW, gb * SSD_STATE
    nblk = ng // gb
    xoff = e // gw
    boff = (2 * e) // st
    coff = boff + nblk
    dtoff = (2 * e + 2 * ng * SSD_STATE) // LANES
    tokw = lambda w, off: pl.BlockSpec((1, c, w), lambda bi, gi, ci, off=off: (bi, ci, off + gi))
    cs = lambda w, off: pl.BlockSpec((1, SSD_CONV - 1, w), lambda bi, gi, ci, off=off: (bi, 0, off + gi))
    cw = lambda w, off: pl.BlockSpec((SSD_CONV, w), lambda bi, gi, ci, off=off: (0, off + gi))
    cbv = lambda w, off: pl.BlockSpec((1, w), lambda bi, gi, ci, off=off: (0, off + gi))
    hv = pl.BlockSpec((1, LANES), lambda bi, gi, ci: (0, 0))
    pad_heads = lambda x: jnp.zeros((1, LANES), F32).at[0, :SSD_HEADS].set(x)
    cb2 = conv_b.reshape(1, -1)
    nh = gb * SSD_RPG
    s0t = jnp.swapaxes(s0, 2, 3)
    stspec = pl.BlockSpec((1, nh, SSD_HEADDIM, SSD_STATE), lambda bi, gi, ci: (bi, gi, 0, 0))
    h, s1 = pl.pallas_call(
        functools.partial(_ssd_kernel, c=c, gb=gb),
        grid=(b, nblk, t // c),
        in_specs=[tokw(gw, 0), tokw(gw, xoff), tokw(st, boff), tokw(st, coff),
                  pl.BlockSpec((1, c, LANES), lambda bi, gi, ci: (bi, ci, dtoff)),
                  cs(gw, 0), cs(st, e // st), cs(st, e // st + nblk),
                  cw(gw, 0), cw(st, e // st), cw(st, e // st + nblk),
                  cbv(gw, 0), cbv(st, e // st), cbv(st, e // st + nblk),
                  hv, hv, cbv(gw, 0), cbv(gw, 0), stspec],
        out_specs=[tokw(gw, 0), stspec],
        out_shape=[jax.ShapeDtypeStruct((b, t, e), BF16), jax.ShapeDtypeStruct(s0t.shape, F32)],
        scratch_shapes=[pltpu.VMEM((8 + c, gw), F32), pltpu.VMEM((8 + c, st), F32), pltpu.VMEM((8 + c, st), F32),
                        pltpu.VMEM((nh, SSD_HEADDIM, SSD_STATE), F32)],
        compiler_params=_params(("parallel", "parallel", "arbitrary")),
        name="ssd",
    )(p, p, p, p, p, conv_state, conv_state, conv_state, conv_w, conv_w, conv_w, cb2, cb2, cb2,
      pad_heads(dt_bias), pad_heads(a_log), jnp.repeat(d_skip, SSD_HEADDIM).reshape(1, e),
      norm_w.reshape(1, e), s0t)
    conv_new = p[:, t - (SSD_CONV - 1):, e:e + e + 2 * ng * SSD_STATE]
    return h.reshape(b * t, e), conv_new, jnp.swapaxes(s1, 2, 3)


def _gla_kernel(*refs, c, hb):
    q_refs, k_refs, v_refs, g_refs = (refs[i * hb:(i + 1) * hb] for i in range(4))
    ad_ref, aup_ref, ab_ref, nw_ref, s0_ref, h_ref, s_out_ref, s_ref = refs[4 * hb:]

    @pl.when(pl.program_id(2) == 0)
    def _():
        s_ref[...] = s0_ref[0]

    lane = lax.broadcasted_iota(jnp.int32, (c, LANES), 1)
    ad = jnp.where(lane < GLA_LORA, ad_ref[0], 0.0)
    la_all = -_softplus(-(_bdot(ad, aup_ref[...]) + ab_ref[...])) / GLA_LOGIT_NORM
    ti = lax.broadcasted_iota(jnp.int32, (c, c), 0)
    si = lax.broadcasted_iota(jnp.int32, (c, c), 1)
    rowi = lax.broadcasted_iota(jnp.int32, (c, GLA_DK), 0)
    eye = (lax.broadcasted_iota(jnp.int32, (GLA_DK, GLA_DK), 0)
           == lax.broadcasted_iota(jnp.int32, (GLA_DK, GLA_DK), 1))
    for j in range(hb):
        lcum = _cumsum_rows(la_all[:, j * GLA_DK:(j + 1) * GLA_DK])
        q = q_refs[j][0] * (GLA_DK ** -0.5)
        k = k_refs[j][0]
        v = v_refs[j][0]
        qe = q * jnp.exp(lcum)
        ke = k * jnp.exp(-lcum)
        att = jnp.where(ti >= si, _bdot_nt(qe, ke), 0.0)
        lend = jnp.sum(jnp.where(rowi == c - 1, lcum, 0.0), axis=0, keepdims=True)
        kend = k * jnp.exp(lend - lcum)
        s = s_ref[j]
        y = _bdot(att, v) + _bdot(qe, s)
        gend_col = jnp.sum(jnp.where(eye, jnp.exp(lend), 0.0), axis=1, keepdims=True)
        s_ref[j] = gend_col * s + _bdot_tn(kend, v)
        yn = y * lax.rsqrt(jnp.mean(y * y, axis=-1, keepdims=True) + NORM_EPS) * nw_ref[...]
        h_ref[0, :, j * GLA_DV:(j + 1) * GLA_DV] = (yn * _silu(g_refs[j][0])).astype(h_ref.dtype)

    @pl.when(pl.program_id(2) == pl.num_programs(2) - 1)
    def _():
        s_out_ref[0] = s_ref[...]


def _gla_mixer(xn, s0, w_in, a_up, a_bias, norm_w):
    b, t, d = xn.shape
    p = _matmul(xn.reshape(b * t, d), w_in).reshape(b, t, -1)
    c = min(t, 16)
    hb = GLA_HEADS
    dk, dv = GLA_DK, GLA_DV
    nblk = GLA_HEADS // hb
    tokw = lambda w, off, j: pl.BlockSpec((1, c, w), lambda bi, hi, ci, off=off, j=j: (bi, ci, off + hi * hb + j))
    heads = lambda w, off: [tokw(w, off, j) for j in range(hb)]
    aup = jnp.zeros((LANES, GLA_KEY), F32).at[:GLA_LORA].set(a_up)
    st = pl.BlockSpec((1, hb, dk, dv), lambda bi, hi, ci: (bi, hi, 0, 0))
    h, s1 = pl.pallas_call(
        functools.partial(_gla_kernel, c=c, hb=hb),
        grid=(b, nblk, t // c),
        in_specs=heads(dk, 0) + heads(dk, GLA_HEADS) + heads(dv, (2 * GLA_KEY) // dv)
        + heads(dv, (2 * GLA_KEY + D_INNER) // dv)
        + [pl.BlockSpec((1, c, LANES), lambda bi, hi, ci: (bi, ci, (2 * GLA_KEY + 2 * D_INNER) // LANES)),
           pl.BlockSpec((LANES, hb * dk), lambda bi, hi, ci: (0, hi)),
           pl.BlockSpec((1, hb * dk), lambda bi, hi, ci: (0, hi)),
           pl.BlockSpec((1, dv), lambda bi, hi, ci: (0, 0)), st],
        out_specs=[pl.BlockSpec((1, c, hb * dv), lambda bi, hi, ci: (bi, ci, hi)), st],
        out_shape=[jax.ShapeDtypeStruct((b, t, D_INNER), BF16), jax.ShapeDtypeStruct(s0.shape, F32)],
        scratch_shapes=[pltpu.VMEM((hb, dk, dv), F32)],
        compiler_params=_params(("parallel", "parallel", "arbitrary")),
        name="gla",
    )(*([p] * (4 * hb + 1)), aup, a_bias.reshape(1, -1), norm_w.reshape(1, -1), s0)
    return h.reshape(b * t, D_INNER), s1


def _trunk(x, pos, states, norm_gains, final_norm, rwkv_p, ret_p, ssd_p, gla_p):
    shift, wkv, ret_s, conv_s, ssd_s, gla_s = states
    b, t, d = x.shape
    x2 = x.reshape(b * t, d)

    def normed(i):
        return _rmsnorm(x2, norm_gains[i], BF16).reshape(b, t, d)

    (w_in, mu, w0, w_up, a0, a_up, k_k, k_a, r_k, ln_w, ln_b, w_out) = rwkv_p
    shift_new = _rmsnorm(x[:, -1], norm_gains[0], F32)
    h, wkv = _rwkv_mixer(normed(0), shift, wkv, w_in, mu, w0, w_up, a0, a_up, k_k, k_a, r_k, ln_w, ln_b)
    w_out_nh = w_out.reshape(RWKV_HEADS, RWKV_HEAD, d).swapaxes(0, 1).reshape(D_INNER, d)
    x2 = _matmul(h, w_out_nh, res=x2, tn=512)
    w_in, w_out = ret_p
    h, ret_s = _ret_mixer(normed(1), pos, ret_s, w_in)
    x2 = _matmul(h, w_out, res=x2, tn=512)
    (w_in, conv_w, conv_b, dt_bias, a_log, d_skip, norm_w, w_out) = ssd_p
    h, conv_s, ssd_s = _ssd_mixer(normed(2), conv_s, ssd_s, w_in, conv_w, conv_b, dt_bias, a_log, d_skip, norm_w)
    x2 = _matmul(h, w_out, res=x2, tn=512)
    (w_in, a_up, a_bias, norm_w, w_out) = gla_p
    h, gla_s = _gla_mixer(normed(3), gla_s, w_in, a_up, a_bias, norm_w)
    x2 = _matmul(h, w_out, res=x2, tn=512)

    y = _rmsnorm(x2, final_norm, F32).reshape(b, t, d)
    return y, (shift_new, wkv, ret_s, conv_s, ssd_s, gla_s)


def kernel(x_prompt, x_sample, state_rwkv_shift, state_rwkv_wkv, state_ret, state_ssd_conv, state_ssd, state_gla, norm_gains, final_norm, rwkv_w_in, rwkv_mu, rwkv_w0, rwkv_w_up, rwkv_a0, rwkv_a_up, rwkv_k_k, rwkv_k_a, rwkv_r_k, rwkv_ln_w, rwkv_ln_b, rwkv_w_out, ret_w_in, ret_w_out, ssd_w_in, ssd_conv_w, ssd_conv_b, ssd_dt_bias, ssd_A_log, ssd_D, ssd_norm_w, ssd_w_out, gla_w_in, gla_a_up, gla_a_bias, gla_norm_w, gla_w_out):
    rwkv_p = (rwkv_w_in, rwkv_mu, rwkv_w0, rwkv_w_up, rwkv_a0, rwkv_a_up, rwkv_k_k, rwkv_k_a, rwkv_r_k,
              rwkv_ln_w, rwkv_ln_b, rwkv_w_out)
    ret_p = (ret_w_in, ret_w_out)
    ssd_p = (ssd_w_in, ssd_conv_w, ssd_conv_b, ssd_dt_bias, ssd_A_log, ssd_D, ssd_norm_w, ssd_w_out)
    gla_p = (gla_w_in, gla_a_up, gla_a_bias, gla_norm_w, gla_w_out)

    bp, tp = x_prompt.shape[0], x_prompt.shape[1]
    zeros = lambda ref: jnp.zeros((bp,) + ref.shape[1:], F32)
    init_prompt = (None, zeros(state_rwkv_wkv), zeros(state_ret), zeros(state_ssd_conv),
                   zeros(state_ssd), zeros(state_gla))
    y_p, st_p = _trunk(x_prompt, jnp.arange(tp), init_prompt, norm_gains, final_norm, rwkv_p, ret_p, ssd_p, gla_p)
    init_sample = (state_rwkv_shift, state_rwkv_wkv, state_ret, state_ssd_conv, state_ssd, state_gla)
    y_s, st_s = _trunk(x_sample, PAST_LEN + jnp.arange(x_sample.shape[1]), init_sample, norm_gains, final_norm,
                       rwkv_p, ret_p, ssd_p, gla_p)
    return (y_p, y_s) + tuple(st_p) + tuple(st_s)
```

```python
import functools

import jax
import jax.numpy as jnp
from jax import lax
from jax.experimental import pallas as pl
from jax.experimental.pallas import tpu as pltpu

F32 = jnp.float32
BF16 = jnp.bfloat16

D_MODEL = 2048
D_INNER = 2 * D_MODEL
NORM_EPS = 1e-6
PAST_LEN = 16384

RWKV_HEAD = 64
RWKV_HEADS = D_INNER // RWKV_HEAD
RWKV_LORA = 96
RWKV_GN_EPS = 64e-5

RET_HEADS = 8
RET_DK = D_MODEL // RET_HEADS
RET_DV = D_INNER // RET_HEADS
ROPE_BASE = 10000.0

SSD_HEADDIM = 64
SSD_HEADS = D_INNER // SSD_HEADDIM
SSD_STATE = 128
SSD_GROUPS = 8
SSD_RPG = SSD_HEADS // SSD_GROUPS
SSD_CONV = 4
SSD_NORM_EPS = 1e-5
SSD_GW = D_INNER // SSD_GROUPS

GLA_HEADS = 4
GLA_KEY = D_MODEL // 2
GLA_DK = GLA_KEY // GLA_HEADS
GLA_DV = D_INNER // GLA_HEADS
GLA_LORA = 16
GLA_LOGIT_NORM = 16.0
GLA_SUB = 16

LANES = 128
VMEM_LIMIT = 52 * 1024 * 1024


def _params(sem):
    return pltpu.CompilerParams(dimension_semantics=sem, vmem_limit_bytes=VMEM_LIMIT)


def _sigmoid(x):
    return 1.0 / (1.0 + jnp.exp(-x))


def _silu(x):
    return x * _sigmoid(x)


def _softplus(x):
    return jnp.maximum(x, 0.0) + jnp.log(1.0 + jnp.exp(-jnp.abs(x)))


def _bdot(a, b):
    return jnp.dot(a.astype(BF16), b.astype(BF16), preferred_element_type=F32)


def _bdot_nt(a, b):
    return lax.dot_general(a.astype(BF16), b.astype(BF16), (((1,), (1,)), ((), ())),
                           preferred_element_type=F32)


def _bdot_tn(a, b):
    return lax.dot_general(a.astype(BF16), b.astype(BF16), (((0,), (0,)), ((), ())),
                           preferred_element_type=F32)


def _cumsum_rows(x):
    c = x.shape[0]
    row = lax.broadcasted_iota(jnp.int32, x.shape, 0)
    if c % 8 == 0:
        s = 1
        while s < c:
            x = x + jnp.where(row >= s, pltpu.roll(x, s, axis=0), 0.0)
            s *= 2
        return x
    acc = jnp.zeros_like(x)
    for j in range(c):
        acc = acc + jnp.where(row >= j, x[j:j + 1, :], 0.0)
    return acc


def _norm_kernel(x_ref, g_ref, o_ref):
    x = x_ref[...]
    ms = jnp.mean(x * x, axis=-1, keepdims=True)
    o_ref[...] = (x * lax.rsqrt(ms + NORM_EPS) * g_ref[...]).astype(o_ref.dtype)


def _rmsnorm(x2d, gain, out_dtype):
    m, d = x2d.shape
    tm = 512 if m % 512 == 0 else m
    return pl.pallas_call(
        _norm_kernel,
        grid=(m // tm,),
        in_specs=[pl.BlockSpec((tm, d), lambda i: (i, 0)), pl.BlockSpec((1, d), lambda i: (0, 0))],
        out_specs=pl.BlockSpec((tm, d), lambda i: (i, 0)),
        out_shape=jax.ShapeDtypeStruct((m, d), out_dtype),
        compiler_params=_params(("parallel",)),
        name="rmsnorm",
    )(x2d, gain.reshape(1, d))


def _mm_kernel(a_ref, w_ref, *rest, has_res):
    if has_res:
        r_ref, o_ref, wb_ref = rest
    else:
        o_ref, wb_ref = rest

    @pl.when(pl.program_id(1) == 0)
    def _():
        wb_ref[...] = w_ref[...].astype(BF16)

    acc = jnp.dot(a_ref[...], wb_ref[...], preferred_element_type=F32)
    if has_res:
        acc = r_ref[...] + acc
    o_ref[...] = acc.astype(o_ref.dtype)


def _matmul(a, w, res=None, tn=1024):
    m, k = a.shape
    n = w.shape[1]
    tm = next((c for c in (1024, 512) if m % c == 0), m)
    tn = min(tn, n)
    grid = (pl.cdiv(n, tn), m // tm)
    in_specs = [pl.BlockSpec((tm, k), lambda j, i: (i, 0)), pl.BlockSpec((k, tn), lambda j, i: (0, j))]
    args = [a, w]
    if res is not None:
        in_specs.append(pl.BlockSpec((tm, tn), lambda j, i: (i, j)))
        args.append(res)
    return pl.pallas_call(
        functools.partial(_mm_kernel, has_res=res is not None),
        grid=grid,
        in_specs=in_specs,
        out_specs=pl.BlockSpec((tm, tn), lambda j, i: (i, j)),
        out_shape=jax.ShapeDtypeStruct((m, n), F32),
        scratch_shapes=[pltpu.VMEM((k, tn), BF16)],
        compiler_params=_params(("arbitrary", "arbitrary")),
        name="matmul",
    )(*args)


def _shifted(buf_ref, carry_ref, x, c):
    if c % 8 == 0:
        row = lax.broadcasted_iota(jnp.int32, x.shape, 0)
        prev = jnp.where(row == 0, carry_ref[...], pltpu.roll(x, 1, axis=0))
        carry_ref[...] = x[c - 1:c, :]
        return prev
    buf_ref[7:8, :] = carry_ref[...]
    buf_ref[8:8 + c, :] = x
    prev = buf_ref[7:7 + c, :]
    carry_ref[...] = buf_ref[8 + c - 1:8 + c, :]
    return prev


def _head_sum(x):
    s = x[:, :LANES]
    for m in range(1, x.shape[1] // LANES):
        s = s + x[:, m * LANES:(m + 1) * LANES]
    return s + pltpu.roll(s, RWKV_HEADS, axis=1)


def _rwkv_kernel(pr_ref, pk_ref, pv_ref, pg_ref, pl_ref, sr_ref, sk_ref, sv_ref, sg_ref, sl_ref,
                 mr_ref, mk_ref, mv_ref, mg_ref, ml_ref, pvec_ref, lnw_ref, lnb_ref, wup_ref, aup_ref, s0_ref,
                 h_ref, s_out_ref,
                 st_ref, buf_ref, bufl_ref, cr_ref, ck_ref, cv_ref, cg_ref, cl_ref,
                 rb, wb, kb, ab, bb, vb, vt_ref, yt_ref, *, c):
    n = RWKV_HEAD
    nk = n // 2

    @pl.when(pl.program_id(1) == 0)
    def _():
        st_ref[...] = s0_ref[0]
        cr_ref[...] = sr_ref[0]
        ck_ref[...] = sk_ref[0]
        cv_ref[...] = sv_ref[0]
        cg_ref[...] = sg_ref[0]
        cl_ref[...] = sl_ref[0]

    def lerp(p_ref, carry_ref, mu_ref, buf):
        p = p_ref[0]
        return p + mu_ref[...] * (_shifted(buf, carry_ref, p, c) - p)

    r = lerp(pr_ref, cr_ref, mr_ref, buf_ref)
    k = lerp(pk_ref, ck_ref, mk_ref, buf_ref)
    v = lerp(pv_ref, cv_ref, mv_ref, buf_ref)
    g = lerp(pg_ref, cg_ref, mg_ref, buf_ref)
    lo = lerp(pl_ref, cl_ref, ml_ref, bufl_ref)
    lane_lo = lax.broadcasted_iota(jnp.int32, lo.shape, 1)
    lo = jnp.where(lane_lo < 2 * RWKV_LORA, lo, 0.0)

    w0, a0, k_k, k_a, r_k = (pvec_ref[i:i + 1, :] for i in range(5))
    decay = jnp.exp(-jnp.exp(-_softplus(-(w0 + _bdot(jnp.tanh(lo), wup_ref[...]))) - 0.5))
    a = _sigmoid(a0 + _bdot(lo, aup_ref[...]))
    kk32 = k * k_k
    nrm = jnp.sqrt(_head_sum(kk32 * kk32))
    kk = kk32 / jnp.tile(jnp.maximum(nrm, 1e-12), (1, nk))
    k2 = k * (1.0 + (a - 1.0) * k_a)
    coef = _head_sum(r * k2 * r_k)
    b = kk * a
    low_half = lax.broadcasted_iota(jnp.int32, (c, LANES), 1) < RWKV_HEADS

    def dup_rows(dst, x):
        for m in range(nk):
            slab = x[:, m * LANES:(m + 1) * LANES]
            swapped = pltpu.roll(slab, RWKV_HEADS, axis=1)
            dst[2 * m] = jnp.where(low_half, slab, swapped)
            dst[2 * m + 1] = jnp.where(low_half, swapped, slab)

    dup_rows(rb, r)
    dup_rows(wb, decay)
    dup_rows(kb, k2)
    dup_rows(ab, -kk)
    dup_rows(bb, b)
    for m in range(nk):
        vb[m] = v[:, m * LANES:(m + 1) * LANES]
    vt_ref[...] = jnp.swapaxes(vb[...], 0, 1)

    def step(t, carry):
        def pass1(kq, sa):
            return sa + st_ref[kq] * ab[kq, pl.ds(t, 1), :]
        sa = lax.fori_loop(0, n, pass1, jnp.zeros((nk, LANES), F32), unroll=16)
        vt = vt_ref[t]

        def pass2(kq, y):
            s_new = (st_ref[kq] * wb[kq, pl.ds(t, 1), :] + sa * bb[kq, pl.ds(t, 1), :]
                     + vt * kb[kq, pl.ds(t, 1), :])
            st_ref[kq] = s_new
            return y + s_new * rb[kq, pl.ds(t, 1), :]
        yt_ref[t] = lax.fori_loop(0, n, pass2, jnp.zeros((nk, LANES), F32), unroll=16)
        return carry

    lax.fori_loop(0, c, step, 0)

    def over_values(x):
        s = jnp.sum(x, axis=0)
        return (s + pltpu.roll(s, RWKV_HEADS, axis=1)) * (1.0 / n)

    y3 = jnp.swapaxes(yt_ref[...], 0, 1)
    d = y3 - over_values(y3)[None]
    var = over_values(d * d)
    hv = d * lax.rsqrt(var + RWKV_GN_EPS)[None] * lnw_ref[...] + lnb_ref[...] + coef[None] * vb[...]
    sg = _silu(g)
    for m in range(nk):
        sl = slice(m * LANES, (m + 1) * LANES)
        h_ref[0, :, sl] = (hv[m] * sg[:, sl]).astype(h_ref.dtype)

    @pl.when(pl.program_id(1) == pl.num_programs(1) - 1)
    def _():
        s_out_ref[0] = st_ref[...]


def _rwkv_mixer(xn, shift, wkv, w_in, mu, w0, w_up, a0, a_up, k_k, k_a, r_k, ln_w, ln_b):
    b, t, d = xn.shape
    hh, n, e = RWKV_HEADS, RWKV_HEAD, D_INNER
    nk = n // 2
    lw = 2 * LANES
    lblk = (4 * e) // lw
    to_nh = lambda x: x.reshape(x.shape[:-1] + (hh, n)).swapaxes(-1, -2)
    flat = lambda x: x.reshape(x.shape[:-2] + (e,))
    w_perm = jnp.concatenate([flat(to_nh(w_in[:, :4 * e].reshape(d, 4, e))).reshape(d, 4 * e), w_in[:, 4 * e:]], axis=1)
    mu_perm = jnp.concatenate([flat(to_nh(mu[:4 * e].reshape(4, e))).reshape(4 * e), mu[4 * e:]]).reshape(1, -1)
    p = _matmul(xn.reshape(b * t, d), w_perm).reshape(b, t, -1)
    if shift is None:
        p0 = jnp.zeros((b, 1, p.shape[-1]), F32)
    else:
        p0 = _matmul(shift.astype(BF16), w_perm).reshape(b, 1, -1)
    wup = jnp.zeros((lw, e), F32).at[:RWKV_LORA].set(flat(to_nh(w_up))).astype(BF16)
    aup = jnp.zeros((lw, e), F32).at[RWKV_LORA:2 * RWKV_LORA].set(flat(to_nh(a_up))).astype(BF16)
    pvec = jnp.stack([flat(to_nh(x)) for x in (w0, a0, k_k, k_a)] + [r_k.T.reshape(e)])
    slabs = lambda x: flat(to_nh(x)).reshape(nk, 1, LANES)
    lnw3, lnb3 = slabs(ln_w), slabs(ln_b)
    s0 = wkv.reshape(b, hh, nk, 2, n).transpose(0, 4, 2, 3, 1).reshape(b, n, nk, LANES)

    c = min(t, 32)
    big = lambda j: pl.BlockSpec((1, c, e), lambda bi, ci, j=j: (bi, ci, j))
    big0 = lambda j: pl.BlockSpec((1, 1, e), lambda bi, ci, j=j: (bi, 0, j))
    mub = lambda j: pl.BlockSpec((1, e), lambda bi, ci, j=j: (0, j))
    const = lambda shp: pl.BlockSpec(shp, lambda bi, ci: (0,) * len(shp))
    st = pl.BlockSpec((1, n, nk, LANES), lambda bi, ci: (bi, 0, 0, 0))
    rows = pltpu.VMEM((n, c, LANES), F32)
    vals = pltpu.VMEM((nk, c, LANES), F32)
    tiles = pltpu.VMEM((c, nk, LANES), F32)
    h, s1 = pl.pallas_call(
        functools.partial(_rwkv_kernel, c=c),
        grid=(b, t // c),
        in_specs=([big(j) for j in range(4)] + [pl.BlockSpec((1, c, lw), lambda bi, ci: (bi, ci, lblk))]
                  + [big0(j) for j in range(4)] + [pl.BlockSpec((1, 1, lw), lambda bi, ci: (bi, 0, lblk))]
                  + [mub(j) for j in range(4)] + [pl.BlockSpec((1, lw), lambda bi, ci: (0, lblk))]
                  + [const((5, e)), const((nk, 1, LANES)), const((nk, 1, LANES)), const((lw, e)), const((lw, e)), st]),
        out_specs=[pl.BlockSpec((1, c, e), lambda bi, ci: (bi, ci, 0)), st],
        out_shape=[jax.ShapeDtypeStruct((b, t, e), BF16), jax.ShapeDtypeStruct((b, n, nk, LANES), F32)],
        scratch_shapes=[pltpu.VMEM((n, nk, LANES), F32), pltpu.VMEM((8 + c, e), F32), pltpu.VMEM((8 + c, lw), F32)]
        + [pltpu.VMEM((1, e), F32)] * 4 + [pltpu.VMEM((1, lw), F32)] + [rows] * 5 + [vals] + [tiles] * 2,
        compiler_params=_params(("parallel", "arbitrary")),
        name="rwkv",
    )(p, p, p, p, p, p0, p0, p0, p0, p0, mu_perm, mu_perm, mu_perm, mu_perm, mu_perm,
      pvec, lnw3, lnb3, wup, aup, s0)
    wkv_new = s1.reshape(b, n, nk, 2, hh).transpose(0, 4, 2, 3, 1).reshape(b, hh, n, n)
    return h.reshape(b * t, e), wkv_new


def _ret_kernel(lg_ref, q_ref, k_ref, v_ref, g_ref, cos_ref, sin_ref, s0_ref, h_ref, s_out_ref, s_ref, *, c, hb):
    @pl.when(pl.program_id(2) == 0)
    def _():
        s_ref[...] = s0_ref[0]

    cos = cos_ref[...]
    sin = sin_ref[...]
    half = RET_DK // 2

    def rot(x):
        x1, x2 = x[:, :half], x[:, half:]
        return jnp.concatenate([x1 * cos - x2 * sin, x1 * sin + x2 * cos], axis=1)

    ti = lax.broadcasted_iota(jnp.int32, (c, c), 0)
    si = lax.broadcasted_iota(jnp.int32, (c, c), 1)
    causal = ti >= si
    dist = (ti - si).astype(F32)
    trow = lax.broadcasted_iota(jnp.int32, (c, RET_DK), 0).astype(F32)
    for j in range(hb):
        lg = lg_ref[pl.program_id(1) * hb + j]
        q = rot(q_ref[0, :, j * RET_DK:(j + 1) * RET_DK])
        k = rot(k_ref[0, :, j * RET_DK:(j + 1) * RET_DK]) * (RET_DK ** -0.5)
        v = v_ref[0, :, j * RET_DV:(j + 1) * RET_DV]
        dmat = jnp.exp(jnp.where(causal, dist * lg, -jnp.inf))
        s = s_ref[j]
        att = _bdot_nt(q, k) * dmat
        y = _bdot(att, v) + _bdot(q * jnp.exp((trow + 1.0) * lg), s)
        s_ref[j] = (jnp.exp(jnp.zeros((1, RET_DV), F32) + lg * c) * s
                    + _bdot_tn(k * jnp.exp((c - 1.0 - trow) * lg), v))
        yn = y * lax.rsqrt(jnp.mean(y * y, axis=-1, keepdims=True) + NORM_EPS)
        h_ref[0, :, j * RET_DV:(j + 1) * RET_DV] = (
            yn * _silu(g_ref[0, :, j * RET_DV:(j + 1) * RET_DV])).astype(h_ref.dtype)

    @pl.when(pl.program_id(2) == pl.num_programs(2) - 1)
    def _():
        s_out_ref[0] = s_ref[...]


def _ret_mixer(xn, pos, s0, w_in):
    b, t, d = xn.shape
    p = _matmul(xn.reshape(b * t, d), w_in).reshape(b, t, -1)
    c = min(t, 256)
    hb = 2 if t > 8 else RET_HEADS
    half = RET_DK // 2
    inv = 1.0 / (ROPE_BASE ** jnp.linspace(0.0, 1.0, half, dtype=F32))
    ang = pos.astype(F32)[:, None] * inv[None, :]
    cos, sin = jnp.cos(ang), jnp.sin(ang)
    log_gamma = jnp.log1p(-(2.0 ** (-5.0 - jnp.arange(RET_HEADS, dtype=F32))))
    nblk = RET_HEADS // hb
    spec = lambda w, off: pl.BlockSpec((1, c, hb * w), lambda bi, hi, ci, lg, off=off: (bi, ci, off + hi))
    tab = pl.BlockSpec((c, half), lambda bi, hi, ci, lg: (ci, 0))
    st = pl.BlockSpec((1, hb, RET_DK, RET_DV), lambda bi, hi, ci, lg: (bi, hi, 0, 0))
    h, s1 = pl.pallas_call(
        functools.partial(_ret_kernel, c=c, hb=hb),
        grid_spec=pltpu.PrefetchScalarGridSpec(
            num_scalar_prefetch=1,
            grid=(b, nblk, t // c),
            in_specs=[spec(RET_DK, 0), spec(RET_DK, nblk), spec(RET_DV, nblk), spec(RET_DV, 2 * nblk), tab, tab, st],
            out_specs=[pl.BlockSpec((1, c, hb * RET_DV), lambda bi, hi, ci, lg: (bi, ci, hi)), st],
            scratch_shapes=[pltpu.VMEM((hb, RET_DK, RET_DV), F32)],
        ),
        out_shape=[jax.ShapeDtypeStruct((b, t, D_INNER), BF16), jax.ShapeDtypeStruct(s0.shape, F32)],
        compiler_params=_params(("parallel", "parallel", "arbitrary")),
        name="retention",
    )(log_gamma, p, p, p, p, cos, sin, s0)
    return h.reshape(b * t, D_INNER), s1


def _causal_conv(buf_ref, x, w, bias, c):
    buf_ref[8:8 + c, :] = x
    acc = bias
    for j in range(SSD_CONV):
        acc = acc + w[j:j + 1, :] * buf_ref[5 + j:5 + j + c, :]
    buf_ref[5:8, :] = buf_ref[5 + c:8 + c, :]
    return acc


def _ssd_kernel(z_ref, x_ref, bm_ref, cm_ref, dt_ref, cx_ref, cb_ref, cc_ref,
                wx_ref, wb_ref, wc_ref, bx_ref, bb_ref, bc_ref, dtb_ref, alog_ref, dsk_ref, nw_ref, s0_ref,
                h_ref, s_out_ref, bufx_ref, bufb_ref, bufc_ref, s_ref, *, c, gb):
    @pl.when(pl.program_id(2) == 0)
    def _():
        s_ref[...] = s0_ref[0]
        bufx_ref[5:8, :] = cx_ref[0]
        bufb_ref[5:8, :] = cb_ref[0]
        bufc_ref[5:8, :] = cc_ref[0]

    xs_all = _silu(_causal_conv(bufx_ref, x_ref[0], wx_ref[...], bx_ref[...], c))
    bm_all = _silu(_causal_conv(bufb_ref, bm_ref[0], wb_ref[...], bb_ref[...], c))
    cm_all = _silu(_causal_conv(bufc_ref, cm_ref[0], wc_ref[...], bc_ref[...], c))

    lane = lax.broadcasted_iota(jnp.int32, (c, LANES), 1)
    dt = jnp.where(lane < SSD_HEADS, _softplus(dt_ref[0] + dtb_ref[...]), 0.0)
    la = dt * (-jnp.exp(alog_ref[...]))
    lcum = _cumsum_rows(la)

    ti = lax.broadcasted_iota(jnp.int32, (c, c), 0)
    si = lax.broadcasted_iota(jnp.int32, (c, c), 1)
    causal = ti >= si
    eye = ti == si
    last = lax.broadcasted_iota(jnp.int32, (c, 1), 0) == c - 1
    gw, ns = SSD_GW, SSD_STATE
    for gl in range(gb):
        xs = xs_all[:, gl * gw:(gl + 1) * gw]
        bm = bm_all[:, gl * ns:(gl + 1) * ns]
        cm = cm_all[:, gl * ns:(gl + 1) * ns]
        scores = _bdot_nt(cm, bm)
        ys = []
        for r in range(SSD_RPG):
            head = (pl.program_id(1) * gb + gl) * SSD_RPG + r
            sel = lane == head
            lcol = jnp.sum(jnp.where(sel, lcum, 0.0), axis=-1, keepdims=True)
            dtc = jnp.sum(jnp.where(sel, dt, 0.0), axis=-1, keepdims=True)
            lrow = jnp.sum(jnp.where(eye, lcol, 0.0), axis=0, keepdims=True)
            lend = jnp.sum(jnp.where(last, lcol, 0.0), axis=0, keepdims=True)
            dec = jnp.exp(jnp.where(causal, lcol - lrow, -jnp.inf))
            v_r = xs[:, r * SSD_HEADDIM:(r + 1) * SSD_HEADDIM] * dtc
            s_r = s_ref[gl * SSD_RPG + r]
            ys.append(_bdot(scores * dec, v_r) + jnp.exp(lcol) * _bdot_nt(cm, s_r))
            s_ref[gl * SSD_RPG + r] = jnp.exp(lend) * s_r + _bdot_tn(jnp.exp(lend - lcol) * v_r, bm)
        sl = slice(gl * gw, (gl + 1) * gw)
        y = jnp.concatenate(ys, axis=1) + xs * dsk_ref[:, sl]
        y = y * _silu(z_ref[0, :, sl])
        yn = y * lax.rsqrt(jnp.mean(y * y, axis=-1, keepdims=True) + SSD_NORM_EPS)
        h_ref[0, :, sl] = (yn * nw_ref[:, sl]).astype(h_ref.dtype)

    @pl.when(pl.program_id(2) == pl.num_programs(2) - 1)
    def _():
        s_out_ref[0] = s_ref[...]


def _ssd_mixer(xn, conv_state, s0, w_in, conv_w, conv_b, dt_bias, a_log, d_skip, norm_w):
    b, t, d = xn.shape
    p = _matmul(xn.reshape(b * t, d), w_in).reshape(b, t, -1)
    c = min(t, 64)
    gb = 2 if t > 8 else SSD_GROUPS
    e, ng = D_INNER, SSD_GROUPS
    gw, st = gb * SSD_GW, gb * SSD_STATE
    nblk = ng // gb
    xoff = e // gw
    boff = (2 * e) // st
    coff = boff + nblk
    dtoff = (2 * e + 2 * ng * SSD_STATE) // LANES
    tokw = lambda w, off: pl.BlockSpec((1, c, w), lambda bi, gi, ci, off=off: (bi, ci, off + gi))
    cs = lambda w, off: pl.BlockSpec((1, SSD_CONV - 1, w), lambda bi, gi, ci, off=off: (bi, 0, off + gi))
    cw = lambda w, off: pl.BlockSpec((SSD_CONV, w), lambda bi, gi, ci, off=off: (0, off + gi))
    cbv = lambda w, off: pl.BlockSpec((1, w), lambda bi, gi, ci, off=off: (0, off + gi))
    hv = pl.BlockSpec((1, LANES), lambda bi, gi, ci: (0, 0))
    pad_heads = lambda x: jnp.zeros((1, LANES), F32).at[0, :SSD_HEADS].set(x)
    cb2 = conv_b.reshape(1, -1)
    nh = gb * SSD_RPG
    s0t = jnp.swapaxes(s0, 2, 3)
    stspec = pl.BlockSpec((1, nh, SSD_HEADDIM, SSD_STATE), lambda bi, gi, ci: (bi, gi, 0, 0))
    h, s1 = pl.pallas_call(
        functools.partial(_ssd_kernel, c=c, gb=gb),
        grid=(b, nblk, t // c),
        in_specs=[tokw(gw, 0), tokw(gw, xoff), tokw(st, boff), tokw(st, coff),
                  pl.BlockSpec((1, c, LANES), lambda bi, gi, ci: (bi, ci, dtoff)),
                  cs(gw, 0), cs(st, e // st), cs(st, e // st + nblk),
                  cw(gw, 0), cw(st, e // st), cw(st, e // st + nblk),
                  cbv(gw, 0), cbv(st, e // st), cbv(st, e // st + nblk),
                  hv, hv, cbv(gw, 0), cbv(gw, 0), stspec],
        out_specs=[tokw(gw, 0), stspec],
        out_shape=[jax.ShapeDtypeStruct((b, t, e), BF16), jax.ShapeDtypeStruct(s0t.shape, F32)],
        scratch_shapes=[pltpu.VMEM((8 + c, gw), F32), pltpu.VMEM((8 + c, st), F32), pltpu.VMEM((8 + c, st), F32),
                        pltpu.VMEM((nh, SSD_HEADDIM, SSD_STATE), F32)],
        compiler_params=_params(("parallel", "parallel", "arbitrary")),
        name="ssd",
    )(p, p, p, p, p, conv_state, conv_state, conv_state, conv_w, conv_w, conv_w, cb2, cb2, cb2,
      pad_heads(dt_bias), pad_heads(a_log), jnp.repeat(d_skip, SSD_HEADDIM).reshape(1, e),
      norm_w.reshape(1, e), s0t)
    conv_new = p[:, t - (SSD_CONV - 1):, e:e + e + 2 * ng * SSD_STATE]
    return h.reshape(b * t, e), conv_new, jnp.swapaxes(s1, 2, 3)


def _gla_kernel(*refs, c, hb):
    q_refs, k_refs, v_refs, g_refs = (refs[i * hb:(i + 1) * hb] for i in range(4))
    ad_ref, aup_ref, ab_ref, nw_ref, s0_ref, h_ref, s_out_ref, s_ref = refs[4 * hb:]

    @pl.when(pl.program_id(2) == 0)
    def _():
        s_ref[...] = s0_ref[0]

    lane = lax.broadcasted_iota(jnp.int32, (c, LANES), 1)
    ad = jnp.where(lane < GLA_LORA, ad_ref[0], 0.0)
    la_all = -_softplus(-(_bdot(ad, aup_ref[...]) + ab_ref[...])) / GLA_LOGIT_NORM
    sub = min(c, GLA_SUB)
    rowi = lax.broadcasted_iota(jnp.int32, (c, GLA_DK), 0)
    eye = (lax.broadcasted_iota(jnp.int32, (GLA_DK, GLA_DK), 0)
           == lax.broadcasted_iota(jnp.int32, (GLA_DK, GLA_DK), 1))
    for j in range(hb):
        lcum = _cumsum_rows(la_all[:, j * GLA_DK:(j + 1) * GLA_DK])
        q = q_refs[j][0] * (GLA_DK ** -0.5)
        k = k_refs[j][0]
        v = v_refs[j][0]
        lend = jnp.sum(jnp.where(rowi == c - 1, lcum, 0.0), axis=0, keepdims=True)
        kend = k * jnp.exp(lend - lcum)
        s = s_ref[j]
        ys = []
        for i in range(c // sub):
            lo, hi = i * sub, (i + 1) * sub
            base = lcum[lo - 1:lo] if i else jnp.zeros((1, GLA_DK), F32)
            lloc = lcum[lo:hi] - base
            qe = q[lo:hi] * jnp.exp(lloc)
            keys = [k[m * sub:(m + 1) * sub] * jnp.exp(base - lcum[m * sub:(m + 1) * sub]) for m in range(i)]
            keys.append(k[lo:hi] * jnp.exp(-lloc))
            att = _bdot_nt(qe, jnp.concatenate(keys, axis=0) if i else keys[0])
            ti = lax.broadcasted_iota(jnp.int32, (sub, hi), 0) + lo
            si = lax.broadcasted_iota(jnp.int32, (sub, hi), 1)
            ys.append(_bdot(jnp.where(ti >= si, att, 0.0), v[:hi]))
        y = (jnp.concatenate(ys, axis=0) if len(ys) > 1 else ys[0]) + _bdot(q * jnp.exp(lcum), s)
        gend_col = jnp.sum(jnp.where(eye, jnp.exp(lend), 0.0), axis=1, keepdims=True)
        s_ref[j] = gend_col * s + _bdot_tn(kend, v)
        yn = y * lax.rsqrt(jnp.mean(y * y, axis=-1, keepdims=True) + NORM_EPS) * nw_ref[...]
        h_ref[0, :, j * GLA_DV:(j + 1) * GLA_DV] = (yn * _silu(g_refs[j][0])).astype(h_ref.dtype)

    @pl.when(pl.program_id(2) == pl.num_programs(2) - 1)
    def _():
        s_out_ref[0] = s_ref[...]


def _gla_mixer(xn, s0, w_in, a_up, a_bias, norm_w):
    b, t, d = xn.shape
    p = _matmul(xn.reshape(b * t, d), w_in).reshape(b, t, -1)
    c = min(t, 4 * GLA_SUB)
    hb = GLA_HEADS
    dk, dv = GLA_DK, GLA_DV
    nblk = GLA_HEADS // hb
    tokw = lambda w, off, j: pl.BlockSpec((1, c, w), lambda bi, hi, ci, off=off, j=j: (bi, ci, off + hi * hb + j))
    heads = lambda w, off: [tokw(w, off, j) for j in range(hb)]
    aup = jnp.zeros((LANES, GLA_KEY), F32).at[:GLA_LORA].set(a_up)
    st = pl.BlockSpec((1, hb, dk, dv), lambda bi, hi, ci: (bi, hi, 0, 0))
    h, s1 = pl.pallas_call(
        functools.partial(_gla_kernel, c=c, hb=hb),
        grid=(b, nblk, t // c),
        in_specs=heads(dk, 0) + heads(dk, GLA_HEADS) + heads(dv, (2 * GLA_KEY) // dv)
        + heads(dv, (2 * GLA_KEY + D_INNER) // dv)
        + [pl.BlockSpec((1, c, LANES), lambda bi, hi, ci: (bi, ci, (2 * GLA_KEY + 2 * D_INNER) // LANES)),
           pl.BlockSpec((LANES, hb * dk), lambda bi, hi, ci: (0, hi)),
           pl.BlockSpec((1, hb * dk), lambda bi, hi, ci: (0, hi)),
           pl.BlockSpec((1, dv), lambda bi, hi, ci: (0, 0)), st],
        out_specs=[pl.BlockSpec((1, c, hb * dv), lambda bi, hi, ci: (bi, ci, hi)), st],
        out_shape=[jax.ShapeDtypeStruct((b, t, D_INNER), BF16), jax.ShapeDtypeStruct(s0.shape, F32)],
        scratch_shapes=[pltpu.VMEM((hb, dk, dv), F32)],
        compiler_params=_params(("parallel", "parallel", "arbitrary")),
        name="gla",
    )(*([p] * (4 * hb + 1)), aup, a_bias.reshape(1, -1), norm_w.reshape(1, -1), s0)
    return h.reshape(b * t, D_INNER), s1


def _trunk(x, pos, states, norm_gains, final_norm, rwkv_p, ret_p, ssd_p, gla_p):
    shift, wkv, ret_s, conv_s, ssd_s, gla_s = states
    b, t, d = x.shape
    x2 = x.reshape(b * t, d)

    def normed(i):
        return _rmsnorm(x2, norm_gains[i], BF16).reshape(b, t, d)

    (w_in, mu, w0, w_up, a0, a_up, k_k, k_a, r_k, ln_w, ln_b, w_out) = rwkv_p
    shift_new = _rmsnorm(x[:, -1], norm_gains[0], F32)
    h, wkv = _rwkv_mixer(normed(0), shift, wkv, w_in, mu, w0, w_up, a0, a_up, k_k, k_a, r_k, ln_w, ln_b)
    w_out_nh = w_out.reshape(RWKV_HEADS, RWKV_HEAD, d).swapaxes(0, 1).reshape(D_INNER, d)
    x2 = _matmul(h, w_out_nh, res=x2, tn=512)
    w_in, w_out = ret_p
    h, ret_s = _ret_mixer(normed(1), pos, ret_s, w_in)
    x2 = _matmul(h, w_out, res=x2, tn=512)
    (w_in, conv_w, conv_b, dt_bias, a_log, d_skip, norm_w, w_out) = ssd_p
    h, conv_s, ssd_s = _ssd_mixer(normed(2), conv_s, ssd_s, w_in, conv_w, conv_b, dt_bias, a_log, d_skip, norm_w)
    x2 = _matmul(h, w_out, res=x2, tn=512)
    (w_in, a_up, a_bias, norm_w, w_out) = gla_p
    h, gla_s = _gla_mixer(normed(3), gla_s, w_in, a_up, a_bias, norm_w)
    x2 = _matmul(h, w_out, res=x2, tn=512)

    y = _rmsnorm(x2, final_norm, F32).reshape(b, t, d)
    return y, (shift_new, wkv, ret_s, conv_s, ssd_s, gla_s)


def kernel(x_prompt, x_sample, state_rwkv_shift, state_rwkv_wkv, state_ret, state_ssd_conv, state_ssd, state_gla, norm_gains, final_norm, rwkv_w_in, rwkv_mu, rwkv_w0, rwkv_w_up, rwkv_a0, rwkv_a_up, rwkv_k_k, rwkv_k_a, rwkv_r_k, rwkv_ln_w, rwkv_ln_b, rwkv_w_out, ret_w_in, ret_w_out, ssd_w_in, ssd_conv_w, ssd_conv_b, ssd_dt_bias, ssd_A_log, ssd_D, ssd_norm_w, ssd_w_out, gla_w_in, gla_a_up, gla_a_bias, gla_norm_w, gla_w_out):
    rwkv_p = (rwkv_w_in, rwkv_mu, rwkv_w0, rwkv_w_up, rwkv_a0, rwkv_a_up, rwkv_k_k, rwkv_k_a, rwkv_r_k,
              rwkv_ln_w, rwkv_ln_b, rwkv_w_out)
    ret_p = (ret_w_in, ret_w_out)
    ssd_p = (ssd_w_in, ssd_conv_w, ssd_conv_b, ssd_dt_bias, ssd_A_log, ssd_D, ssd_norm_w, ssd_w_out)
    gla_p = (gla_w_in, gla_a_up, gla_a_bias, gla_norm_w, gla_w_out)

    bp, tp = x_prompt.shape[0], x_prompt.shape[1]
    zeros = lambda ref: jnp.zeros((bp,) + ref.shape[1:], F32)
    init_prompt = (None, zeros(state_rwkv_wkv), zeros(state_ret), zeros(state_ssd_conv),
                   zeros(state_ssd), zeros(state_gla))
    y_p, st_p = _trunk(x_prompt, jnp.arange(tp), init_prompt, norm_gains, final_norm, rwkv_p, ret_p, ssd_p, gla_p)
    init_sample = (state_rwkv_shift, state_rwkv_wkv, state_ret, state_ssd_conv, state_ssd, state_gla)
    y_s, st_s = _trunk(x_sample, PAST_LEN + jnp.arange(x_sample.shape[1]), init_sample, norm_gains, final_norm,
                       rwkv_p, ret_p, ssd_p, gla_p)
    return (y_p, y_s) + tuple(st_p) + tuple(st_s)
```

```python
import functools

import jax
import jax.numpy as jnp
from jax import lax
from jax.experimental import pallas as pl
from jax.experimental.pallas import tpu as pltpu

F32 = jnp.float32
BF16 = jnp.bfloat16

D_MODEL = 2048
D_INNER = 2 * D_MODEL
NORM_EPS = 1e-6
PAST_LEN = 16384

RWKV_HEAD = 64
RWKV_HEADS = D_INNER // RWKV_HEAD
RWKV_LORA = 96
RWKV_GN_EPS = 64e-5

RET_HEADS = 8
RET_DK = D_MODEL // RET_HEADS
RET_DV = D_INNER // RET_HEADS
ROPE_BASE = 10000.0

SSD_HEADDIM = 64
SSD_HEADS = D_INNER // SSD_HEADDIM
SSD_STATE = 128
SSD_GROUPS = 8
SSD_RPG = SSD_HEADS // SSD_GROUPS
SSD_CONV = 4
SSD_NORM_EPS = 1e-5
SSD_GW = D_INNER // SSD_GROUPS

GLA_HEADS = 4
GLA_KEY = D_MODEL // 2
GLA_DK = GLA_KEY // GLA_HEADS
GLA_DV = D_INNER // GLA_HEADS
GLA_LORA = 16
GLA_LOGIT_NORM = 16.0
GLA_SUB = 16

LANES = 128
VMEM_LIMIT = 52 * 1024 * 1024


def _params(sem):
    return pltpu.CompilerParams(dimension_semantics=sem, vmem_limit_bytes=VMEM_LIMIT)


def _sigmoid(x):
    return 1.0 / (1.0 + jnp.exp(-x))


def _silu(x):
    return x * _sigmoid(x)


def _softplus(x):
    return jnp.maximum(x, 0.0) + jnp.log(1.0 + jnp.exp(-jnp.abs(x)))


def _bdot(a, b):
    return jnp.dot(a.astype(BF16), b.astype(BF16), preferred_element_type=F32)


def _bdot_nt(a, b):
    return lax.dot_general(a.astype(BF16), b.astype(BF16), (((1,), (1,)), ((), ())),
                           preferred_element_type=F32)


def _bdot_tn(a, b):
    return lax.dot_general(a.astype(BF16), b.astype(BF16), (((0,), (0,)), ((), ())),
                           preferred_element_type=F32)


def _cumsum_rows(x):
    c = x.shape[0]
    row = lax.broadcasted_iota(jnp.int32, x.shape, 0)
    if c % 8 == 0:
        s = 1
        while s < c:
            x = x + jnp.where(row >= s, pltpu.roll(x, s, axis=0), 0.0)
            s *= 2
        return x
    acc = jnp.zeros_like(x)
    for j in range(c):
        acc = acc + jnp.where(row >= j, x[j:j + 1, :], 0.0)
    return acc


def _norm_kernel(x_ref, g_ref, o_ref):
    x = x_ref[...]
    ms = jnp.mean(x * x, axis=-1, keepdims=True)
    o_ref[...] = (x * lax.rsqrt(ms + NORM_EPS) * g_ref[...]).astype(o_ref.dtype)


def _rmsnorm(x2d, gain, out_dtype):
    m, d = x2d.shape
    tm = 512 if m % 512 == 0 else m
    return pl.pallas_call(
        _norm_kernel,
        grid=(m // tm,),
        in_specs=[pl.BlockSpec((tm, d), lambda i: (i, 0)), pl.BlockSpec((1, d), lambda i: (0, 0))],
        out_specs=pl.BlockSpec((tm, d), lambda i: (i, 0)),
        out_shape=jax.ShapeDtypeStruct((m, d), out_dtype),
        compiler_params=_params(("parallel",)),
        name="rmsnorm",
    )(x2d, gain.reshape(1, d))


def _mm_kernel(a_ref, w_ref, *rest, has_res):
    if has_res:
        r_ref, o_ref, wb_ref = rest
    else:
        o_ref, wb_ref = rest

    @pl.when(pl.program_id(1) == 0)
    def _():
        wb_ref[...] = w_ref[...].astype(BF16)

    acc = jnp.dot(a_ref[...], wb_ref[...], preferred_element_type=F32)
    if has_res:
        acc = r_ref[...] + acc
    o_ref[...] = acc.astype(o_ref.dtype)


def _matmul(a, w, res=None, tn=1024):
    m, k = a.shape
    n = w.shape[1]
    tm = next((c for c in (1024, 512) if m % c == 0), m)
    tn = min(tn, n)
    grid = (pl.cdiv(n, tn), m // tm)
    in_specs = [pl.BlockSpec((tm, k), lambda j, i: (i, 0)), pl.BlockSpec((k, tn), lambda j, i: (0, j))]
    args = [a, w]
    if res is not None:
        in_specs.append(pl.BlockSpec((tm, tn), lambda j, i: (i, j)))
        args.append(res)
    return pl.pallas_call(
        functools.partial(_mm_kernel, has_res=res is not None),
        grid=grid,
        in_specs=in_specs,
        out_specs=pl.BlockSpec((tm, tn), lambda j, i: (i, j)),
        out_shape=jax.ShapeDtypeStruct((m, n), F32),
        scratch_shapes=[pltpu.VMEM((k, tn), BF16)],
        compiler_params=_params(("arbitrary", "arbitrary")),
        name="matmul",
    )(*args)


def _shifted(buf_ref, carry_ref, x, c):
    if c % 8 == 0:
        row = lax.broadcasted_iota(jnp.int32, x.shape, 0)
        prev = jnp.where(row == 0, carry_ref[...], pltpu.roll(x, 1, axis=0))
        carry_ref[...] = x[c - 1:c, :]
        return prev
    buf_ref[7:8, :] = carry_ref[...]
    buf_ref[8:8 + c, :] = x
    prev = buf_ref[7:7 + c, :]
    carry_ref[...] = buf_ref[8 + c - 1:8 + c, :]
    return prev


def _head_sum(x):
    s = x[:, :LANES]
    for m in range(1, x.shape[1] // LANES):
        s = s + x[:, m * LANES:(m + 1) * LANES]
    return s + pltpu.roll(s, RWKV_HEADS, axis=1)


def _rwkv_kernel(pr_ref, pk_ref, pv_ref, pg_ref, pl_ref, sr_ref, sk_ref, sv_ref, sg_ref, sl_ref,
                 mr_ref, mk_ref, mv_ref, mg_ref, ml_ref, pvec_ref, lnw_ref, lnb_ref, wup_ref, aup_ref, s0_ref,
                 h_ref, s_out_ref,
                 st_ref, buf_ref, bufl_ref, cr_ref, ck_ref, cv_ref, cg_ref, cl_ref,
                 rb, wb, kb, ab, bb, vb, vt_ref, yt_ref, *, c):
    n = RWKV_HEAD
    nk = n // 2

    @pl.when(pl.program_id(1) == 0)
    def _():
        st_ref[...] = s0_ref[0]
        cr_ref[...] = sr_ref[0]
        ck_ref[...] = sk_ref[0]
        cv_ref[...] = sv_ref[0]
        cg_ref[...] = sg_ref[0]
        cl_ref[...] = sl_ref[0]

    def lerp(p_ref, carry_ref, mu_ref, buf):
        p = p_ref[0]
        return p + mu_ref[...] * (_shifted(buf, carry_ref, p, c) - p)

    low_half = lax.broadcasted_iota(jnp.int32, (c, LANES), 1) < RWKV_HEADS

    def dup_rows(dst, x):
        for m in range(nk):
            slab = x[:, m * LANES:(m + 1) * LANES]
            swapped = pltpu.roll(slab, RWKV_HEADS, axis=1)
            dst[2 * m] = jnp.where(low_half, slab, swapped)
            dst[2 * m + 1] = jnp.where(low_half, swapped, slab)

    w0, a0, k_k, k_a, r_k = (pvec_ref[i:i + 1, :] for i in range(5))
    lo = lerp(pl_ref, cl_ref, ml_ref, bufl_ref)
    lane_lo = lax.broadcasted_iota(jnp.int32, lo.shape, 1)
    lo = jnp.where(lane_lo < 2 * RWKV_LORA, lo, 0.0)
    dup_rows(wb, jnp.exp(-jnp.exp(-_softplus(-(w0 + _bdot(jnp.tanh(lo), wup_ref[...]))) - 0.5)))
    a = _sigmoid(a0 + _bdot(lo, aup_ref[...]))
    k = lerp(pk_ref, ck_ref, mk_ref, buf_ref)
    kk32 = k * k_k
    nrm = jnp.sqrt(_head_sum(kk32 * kk32))
    kk = kk32 / jnp.tile(jnp.maximum(nrm, 1e-12), (1, nk))
    dup_rows(ab, -kk)
    dup_rows(bb, kk * a)
    k2 = k * (1.0 + (a - 1.0) * k_a)
    dup_rows(kb, k2)
    r = lerp(pr_ref, cr_ref, mr_ref, buf_ref)
    coef = _head_sum(r * k2 * r_k)
    dup_rows(rb, r)
    v = lerp(pv_ref, cv_ref, mv_ref, buf_ref)
    for m in range(nk):
        vb[m] = v[:, m * LANES:(m + 1) * LANES]
    vt_ref[...] = jnp.swapaxes(vb[...], 0, 1)

    def step(t, carry):
        def pass1(kq, sa):
            return sa + st_ref[kq] * ab[kq, pl.ds(t, 1), :]
        sa = lax.fori_loop(0, n, pass1, jnp.zeros((nk, LANES), F32), unroll=32)
        vt = vt_ref[t]

        def pass2(kq, y):
            s_new = (st_ref[kq] * wb[kq, pl.ds(t, 1), :] + sa * bb[kq, pl.ds(t, 1), :]
                     + vt * kb[kq, pl.ds(t, 1), :])
            st_ref[kq] = s_new
            return y + s_new * rb[kq, pl.ds(t, 1), :]
        yt_ref[t] = lax.fori_loop(0, n, pass2, jnp.zeros((nk, LANES), F32), unroll=32)
        return carry

    lax.fori_loop(0, c, step, 0)

    def over_values(x):
        s = jnp.sum(x, axis=0)
        return (s + pltpu.roll(s, RWKV_HEADS, axis=1)) * (1.0 / n)

    y3 = jnp.swapaxes(yt_ref[...], 0, 1)
    d = y3 - over_values(y3)[None]
    var = over_values(d * d)
    hv = d * lax.rsqrt(var + RWKV_GN_EPS)[None] * lnw_ref[...] + lnb_ref[...] + coef[None] * vb[...]
    sg = _silu(lerp(pg_ref, cg_ref, mg_ref, buf_ref))
    for m in range(nk):
        sl = slice(m * LANES, (m + 1) * LANES)
        h_ref[0, :, sl] = (hv[m] * sg[:, sl]).astype(h_ref.dtype)

    @pl.when(pl.program_id(1) == pl.num_programs(1) - 1)
    def _():
        s_out_ref[0] = st_ref[...]


def _rwkv_mixer(xn, shift, wkv, w_in, mu, w0, w_up, a0, a_up, k_k, k_a, r_k, ln_w, ln_b):
    b, t, d = xn.shape
    hh, n, e = RWKV_HEADS, RWKV_HEAD, D_INNER
    nk = n // 2
    lw = 2 * LANES
    lblk = (4 * e) // lw
    to_nh = lambda x: x.reshape(x.shape[:-1] + (hh, n)).swapaxes(-1, -2)
    flat = lambda x: x.reshape(x.shape[:-2] + (e,))
    w_perm = jnp.concatenate([flat(to_nh(w_in[:, :4 * e].reshape(d, 4, e))).reshape(d, 4 * e), w_in[:, 4 * e:]], axis=1)
    mu_perm = jnp.concatenate([flat(to_nh(mu[:4 * e].reshape(4, e))).reshape(4 * e), mu[4 * e:]]).reshape(1, -1)
    p = _matmul(xn.reshape(b * t, d), w_perm).reshape(b, t, -1)
    if shift is None:
        p0 = jnp.zeros((b, 1, p.shape[-1]), F32)
    else:
        p0 = _matmul(shift.astype(BF16), w_perm).reshape(b, 1, -1)
    wup = jnp.zeros((lw, e), F32).at[:RWKV_LORA].set(flat(to_nh(w_up))).astype(BF16)
    aup = jnp.zeros((lw, e), F32).at[RWKV_LORA:2 * RWKV_LORA].set(flat(to_nh(a_up))).astype(BF16)
    pvec = jnp.stack([flat(to_nh(x)) for x in (w0, a0, k_k, k_a)] + [r_k.T.reshape(e)])
    slabs = lambda x: flat(to_nh(x)).reshape(nk, 1, LANES)
    lnw3, lnb3 = slabs(ln_w), slabs(ln_b)
    s0 = wkv.reshape(b, hh, nk, 2, n).transpose(0, 4, 2, 3, 1).reshape(b, n, nk, LANES)

    c = min(t, 32)
    big = lambda j: pl.BlockSpec((1, c, e), lambda bi, ci, j=j: (bi, ci, j))
    big0 = lambda j: pl.BlockSpec((1, 1, e), lambda bi, ci, j=j: (bi, 0, j))
    mub = lambda j: pl.BlockSpec((1, e), lambda bi, ci, j=j: (0, j))
    const = lambda shp: pl.BlockSpec(shp, lambda bi, ci: (0,) * len(shp))
    st = pl.BlockSpec((1, n, nk, LANES), lambda bi, ci: (bi, 0, 0, 0))
    rows = pltpu.VMEM((n, c, LANES), F32)
    vals = pltpu.VMEM((nk, c, LANES), F32)
    tiles = pltpu.VMEM((c, nk, LANES), F32)
    h, s1 = pl.pallas_call(
        functools.partial(_rwkv_kernel, c=c),
        grid=(b, t // c),
        in_specs=([big(j) for j in range(4)] + [pl.BlockSpec((1, c, lw), lambda bi, ci: (bi, ci, lblk))]
                  + [big0(j) for j in range(4)] + [pl.BlockSpec((1, 1, lw), lambda bi, ci: (bi, 0, lblk))]
                  + [mub(j) for j in range(4)] + [pl.BlockSpec((1, lw), lambda bi, ci: (0, lblk))]
                  + [const((5, e)), const((nk, 1, LANES)), const((nk, 1, LANES)), const((lw, e)), const((lw, e)), st]),
        out_specs=[pl.BlockSpec((1, c, e), lambda bi, ci: (bi, ci, 0)), st],
        out_shape=[jax.ShapeDtypeStruct((b, t, e), BF16), jax.ShapeDtypeStruct((b, n, nk, LANES), F32)],
        scratch_shapes=[pltpu.VMEM((n, nk, LANES), F32), pltpu.VMEM((8 + c, e), F32), pltpu.VMEM((8 + c, lw), F32)]
        + [pltpu.VMEM((1, e), F32)] * 4 + [pltpu.VMEM((1, lw), F32)] + [rows] * 5 + [vals] + [tiles] * 2,
        compiler_params=_params(("parallel", "arbitrary")),
        name="rwkv",
    )(p, p, p, p, p, p0, p0, p0, p0, p0, mu_perm, mu_perm, mu_perm, mu_perm, mu_perm,
      pvec, lnw3, lnb3, wup, aup, s0)
    wkv_new = s1.reshape(b, n, nk, 2, hh).transpose(0, 4, 2, 3, 1).reshape(b, hh, n, n)
    return h.reshape(b * t, e), wkv_new


def _ret_kernel(lg_ref, q_ref, k_ref, v_ref, g_ref, cos_ref, sin_ref, s0_ref, h_ref, s_out_ref, s_ref, *, c, hb):
    @pl.when(pl.program_id(2) == 0)
    def _():
        s_ref[...] = s0_ref[0]

    cos = cos_ref[...]
    sin = sin_ref[...]
    half = RET_DK // 2

    def rot(x):
        x1, x2 = x[:, :half], x[:, half:]
        return jnp.concatenate([x1 * cos - x2 * sin, x1 * sin + x2 * cos], axis=1)

    ti = lax.broadcasted_iota(jnp.int32, (c, c), 0)
    si = lax.broadcasted_iota(jnp.int32, (c, c), 1)
    causal = ti >= si
    dist = (ti - si).astype(F32)
    trow = lax.broadcasted_iota(jnp.int32, (c, RET_DK), 0).astype(F32)
    for j in range(hb):
        lg = lg_ref[pl.program_id(1) * hb + j]
        q = rot(q_ref[0, :, j * RET_DK:(j + 1) * RET_DK])
        k = rot(k_ref[0, :, j * RET_DK:(j + 1) * RET_DK]) * (RET_DK ** -0.5)
        v = v_ref[0, :, j * RET_DV:(j + 1) * RET_DV]
        dmat = jnp.exp(jnp.where(causal, dist * lg, -jnp.inf))
        s = s_ref[j]
        att = _bdot_nt(q, k) * dmat
        y = _bdot(att, v) + _bdot(q * jnp.exp((trow + 1.0) * lg), s)
        s_ref[j] = (jnp.exp(jnp.zeros((1, RET_DV), F32) + lg * c) * s
                    + _bdot_tn(k * jnp.exp((c - 1.0 - trow) * lg), v))
        yn = y * lax.rsqrt(jnp.mean(y * y, axis=-1, keepdims=True) + NORM_EPS)
        h_ref[0, :, j * RET_DV:(j + 1) * RET_DV] = (
            yn * _silu(g_ref[0, :, j * RET_DV:(j + 1) * RET_DV])).astype(h_ref.dtype)

    @pl.when(pl.program_id(2) == pl.num_programs(2) - 1)
    def _():
        s_out_ref[0] = s_ref[...]


def _ret_mixer(xn, pos, s0, w_in):
    b, t, d = xn.shape
    p = _matmul(xn.reshape(b * t, d), w_in).reshape(b, t, -1)
    c = min(t, 256)
    hb = 4 if t > 8 else RET_HEADS
    half = RET_DK // 2
    inv = 1.0 / (ROPE_BASE ** jnp.linspace(0.0, 1.0, half, dtype=F32))
    ang = pos.astype(F32)[:, None] * inv[None, :]
    cos, sin = jnp.cos(ang), jnp.sin(ang)
    log_gamma = jnp.log1p(-(2.0 ** (-5.0 - jnp.arange(RET_HEADS, dtype=F32))))
    nblk = RET_HEADS // hb
    spec = lambda w, off: pl.BlockSpec((1, c, hb * w), lambda bi, hi, ci, lg, off=off: (bi, ci, off + hi))
    tab = pl.BlockSpec((c, half), lambda bi, hi, ci, lg: (ci, 0))
    st = pl.BlockSpec((1, hb, RET_DK, RET_DV), lambda bi, hi, ci, lg: (bi, hi, 0, 0))
    h, s1 = pl.pallas_call(
        functools.partial(_ret_kernel, c=c, hb=hb),
        grid_spec=pltpu.PrefetchScalarGridSpec(
            num_scalar_prefetch=1,
            grid=(b, nblk, t // c),
            in_specs=[spec(RET_DK, 0), spec(RET_DK, nblk), spec(RET_DV, nblk), spec(RET_DV, 2 * nblk), tab, tab, st],
            out_specs=[pl.BlockSpec((1, c, hb * RET_DV), lambda bi, hi, ci, lg: (bi, ci, hi)), st],
            scratch_shapes=[pltpu.VMEM((hb, RET_DK, RET_DV), F32)],
        ),
        out_shape=[jax.ShapeDtypeStruct((b, t, D_INNER), BF16), jax.ShapeDtypeStruct(s0.shape, F32)],
        compiler_params=_params(("parallel", "parallel", "arbitrary")),
        name="retention",
    )(log_gamma, p, p, p, p, cos, sin, s0)
    return h.reshape(b * t, D_INNER), s1


def _causal_conv(buf_ref, x, w, bias, c):
    buf_ref[8:8 + c, :] = x
    acc = bias
    for j in range(SSD_CONV):
        acc = acc + w[j:j + 1, :] * buf_ref[5 + j:5 + j + c, :]
    buf_ref[5:8, :] = buf_ref[5 + c:8 + c, :]
    return acc


def _ssd_kernel(z_ref, x_ref, bm_ref, cm_ref, dt_ref, cx_ref, cb_ref, cc_ref,
                wx_ref, wb_ref, wc_ref, bx_ref, bb_ref, bc_ref, dtb_ref, alog_ref, dsk_ref, nw_ref, s0_ref,
                h_ref, s_out_ref, bufx_ref, bufb_ref, bufc_ref, s_ref, *, c, gb):
    @pl.when(pl.program_id(2) == 0)
    def _():
        s_ref[...] = s0_ref[0]
        bufx_ref[5:8, :] = cx_ref[0]
        bufb_ref[5:8, :] = cb_ref[0]
        bufc_ref[5:8, :] = cc_ref[0]

    xs_all = _silu(_causal_conv(bufx_ref, x_ref[0], wx_ref[...], bx_ref[...], c))
    bm_all = _silu(_causal_conv(bufb_ref, bm_ref[0], wb_ref[...], bb_ref[...], c))
    cm_all = _silu(_causal_conv(bufc_ref, cm_ref[0], wc_ref[...], bc_ref[...], c))

    lane = lax.broadcasted_iota(jnp.int32, (c, LANES), 1)
    dt = jnp.where(lane < SSD_HEADS, _softplus(dt_ref[0] + dtb_ref[...]), 0.0)
    la = dt * (-jnp.exp(alog_ref[...]))
    lcum = _cumsum_rows(la)

    ti = lax.broadcasted_iota(jnp.int32, (c, c), 0)
    si = lax.broadcasted_iota(jnp.int32, (c, c), 1)
    causal = ti >= si
    eye = ti == si
    last = lax.broadcasted_iota(jnp.int32, (c, 1), 0) == c - 1
    gw, ns = SSD_GW, SSD_STATE
    for gl in range(gb):
        xs = xs_all[:, gl * gw:(gl + 1) * gw]
        bm = bm_all[:, gl * ns:(gl + 1) * ns]
        cm = cm_all[:, gl * ns:(gl + 1) * ns]
        scores = _bdot_nt(cm, bm)
        ys = []
        for r in range(SSD_RPG):
            head = (pl.program_id(1) * gb + gl) * SSD_RPG + r
            sel = lane == head
            lcol = jnp.sum(jnp.where(sel, lcum, 0.0), axis=-1, keepdims=True)
            dtc = jnp.sum(jnp.where(sel, dt, 0.0), axis=-1, keepdims=True)
            lrow = jnp.sum(jnp.where(eye, lcol, 0.0), axis=0, keepdims=True)
            lend = jnp.sum(jnp.where(last, lcol, 0.0), axis=0, keepdims=True)
            dec = jnp.exp(jnp.where(causal, lcol - lrow, -jnp.inf))
            v_r = xs[:, r * SSD_HEADDIM:(r + 1) * SSD_HEADDIM] * dtc
            s_r = s_ref[gl * SSD_RPG + r]
            ys.append(_bdot(scores * dec, v_r) + jnp.exp(lcol) * _bdot_nt(cm, s_r))
            s_ref[gl * SSD_RPG + r] = jnp.exp(lend) * s_r + _bdot_tn(jnp.exp(lend - lcol) * v_r, bm)
        sl = slice(gl * gw, (gl + 1) * gw)
        y = jnp.concatenate(ys, axis=1) + xs * dsk_ref[:, sl]
        y = y * _silu(z_ref[0, :, sl])
        yn = y * lax.rsqrt(jnp.mean(y * y, axis=-1, keepdims=True) + SSD_NORM_EPS)
        h_ref[0, :, sl] = (yn * nw_ref[:, sl]).astype(h_ref.dtype)

    @pl.when(pl.program_id(2) == pl.num_programs(2) - 1)
    def _():
        s_out_ref[0] = s_ref[...]


def _ssd_mixer(xn, conv_state, s0, w_in, conv_w, conv_b, dt_bias, a_log, d_skip, norm_w):
    b, t, d = xn.shape
    p = _matmul(xn.reshape(b * t, d), w_in).reshape(b, t, -1)
    c = min(t, 64)
    gb = 4 if t > 8 else SSD_GROUPS
    e, ng = D_INNER, SSD_GROUPS
    gw, st = gb * SSD_GW, gb * SSD_STATE
    nblk = ng // gb
    xoff = e // gw
    boff = (2 * e) // st
    coff = boff + nblk
    dtoff = (2 * e + 2 * ng * SSD_STATE) // LANES
    tokw = lambda w, off: pl.BlockSpec((1, c, w), lambda bi, gi, ci, off=off: (bi, ci, off + gi))
    cs = lambda w, off: pl.BlockSpec((1, SSD_CONV - 1, w), lambda bi, gi, ci, off=off: (bi, 0, off + gi))
    cw = lambda w, off: pl.BlockSpec((SSD_CONV, w), lambda bi, gi, ci, off=off: (0, off + gi))
    cbv = lambda w, off: pl.BlockSpec((1, w), lambda bi, gi, ci, off=off: (0, off + gi))
    hv = pl.BlockSpec((1, LANES), lambda bi, gi, ci: (0, 0))
    pad_heads = lambda x: jnp.zeros((1, LANES), F32).at[0, :SSD_HEADS].set(x)
    cb2 = conv_b.reshape(1, -1)
    nh = gb * SSD_RPG
    s0t = jnp.swapaxes(s0, 2, 3)
    stspec = pl.BlockSpec((1, nh, SSD_HEADDIM, SSD_STATE), lambda bi, gi, ci: (bi, gi, 0, 0))
    h, s1 = pl.pallas_call(
        functools.partial(_ssd_kernel, c=c, gb=gb),
        grid=(b, nblk, t // c),
        in_specs=[tokw(gw, 0), tokw(gw, xoff), tokw(st, boff), tokw(st, coff),
                  pl.BlockSpec((1, c, LANES), lambda bi, gi, ci: (bi, ci, dtoff)),
                  cs(gw, 0), cs(st, e // st), cs(st, e // st + nblk),
                  cw(gw, 0), cw(st, e // st), cw(st, e // st + nblk),
                  cbv(gw, 0), cbv(st, e // st), cbv(st, e // st + nblk),
                  hv, hv, cbv(gw, 0), cbv(gw, 0), stspec],
        out_specs=[tokw(gw, 0), stspec],
        out_shape=[jax.ShapeDtypeStruct((b, t, e), BF16), jax.ShapeDtypeStruct(s0t.shape, F32)],
        scratch_shapes=[pltpu.VMEM((8 + c, gw), F32), pltpu.VMEM((8 + c, st), F32), pltpu.VMEM((8 + c, st), F32),
                        pltpu.VMEM((nh, SSD_HEADDIM, SSD_STATE), F32)],
        compiler_params=_params(("parallel", "parallel", "arbitrary")),
        name="ssd",
    )(p, p, p, p, p, conv_state, conv_state, conv_state, conv_w, conv_w, conv_w, cb2, cb2, cb2,
      pad_heads(dt_bias), pad_heads(a_log), jnp.repeat(d_skip, SSD_HEADDIM).reshape(1, e),
      norm_w.reshape(1, e), s0t)
    conv_new = p[:, t - (SSD_CONV - 1):, e:e + e + 2 * ng * SSD_STATE]
    return h.reshape(b * t, e), conv_new, jnp.swapaxes(s1, 2, 3)


def _gla_kernel(*refs, c, hb):
    q_refs, k_refs, v_refs, g_refs = (refs[i * hb:(i + 1) * hb] for i in range(4))
    ad_ref, aup_ref, ab_ref, nw_ref, s0_ref, h_ref, s_out_ref, s_ref = refs[4 * hb:]

    @pl.when(pl.program_id(2) == 0)
    def _():
        s_ref[...] = s0_ref[0]

    lane = lax.broadcasted_iota(jnp.int32, (c, LANES), 1)
    ad = jnp.where(lane < GLA_LORA, ad_ref[0], 0.0)
    la_all = -_softplus(-(_bdot(ad, aup_ref[...]) + ab_ref[...])) / GLA_LOGIT_NORM
    sub = min(c, GLA_SUB)
    rowi = lax.broadcasted_iota(jnp.int32, (c, GLA_DK), 0)
    eye = (lax.broadcasted_iota(jnp.int32, (GLA_DK, GLA_DK), 0)
           == lax.broadcasted_iota(jnp.int32, (GLA_DK, GLA_DK), 1))
    for j in range(hb):
        lcum = _cumsum_rows(la_all[:, j * GLA_DK:(j + 1) * GLA_DK])
        q = q_refs[j][0] * (GLA_DK ** -0.5)
        k = k_refs[j][0]
        v = v_refs[j][0]
        lend = jnp.sum(jnp.where(rowi == c - 1, lcum, 0.0), axis=0, keepdims=True)
        kend = k * jnp.exp(lend - lcum)
        s = s_ref[j]
        ys = []
        for i in range(c // sub):
            lo, hi = i * sub, (i + 1) * sub
            base = lcum[lo - 1:lo] if i else jnp.zeros((1, GLA_DK), F32)
            lloc = lcum[lo:hi] - base
            qe = q[lo:hi] * jnp.exp(lloc)
            keys = [k[m * sub:(m + 1) * sub] * jnp.exp(base - lcum[m * sub:(m + 1) * sub]) for m in range(i)]
            keys.append(k[lo:hi] * jnp.exp(-lloc))
            att = _bdot_nt(qe, jnp.concatenate(keys, axis=0) if i else keys[0])
            ti = lax.broadcasted_iota(jnp.int32, (sub, hi), 0) + lo
            si = lax.broadcasted_iota(jnp.int32, (sub, hi), 1)
            ys.append(_bdot(jnp.where(ti >= si, att, 0.0), v[:hi]))
        y = (jnp.concatenate(ys, axis=0) if len(ys) > 1 else ys[0]) + _bdot(q * jnp.exp(lcum), s)
        gend_col = jnp.sum(jnp.where(eye, jnp.exp(lend), 0.0), axis=1, keepdims=True)
        s_ref[j] = gend_col * s + _bdot_tn(kend, v)
        yn = y * lax.rsqrt(jnp.mean(y * y, axis=-1, keepdims=True) + NORM_EPS) * nw_ref[...]
        h_ref[0, :, j * GLA_DV:(j + 1) * GLA_DV] = (yn * _silu(g_refs[j][0])).astype(h_ref.dtype)

    @pl.when(pl.program_id(2) == pl.num_programs(2) - 1)
    def _():
        s_out_ref[0] = s_ref[...]


def _gla_mixer(xn, s0, w_in, a_up, a_bias, norm_w):
    b, t, d = xn.shape
    p = _matmul(xn.reshape(b * t, d), w_in).reshape(b, t, -1)
    c = min(t, 4 * GLA_SUB)
    hb = GLA_HEADS
    dk, dv = GLA_DK, GLA_DV
    nblk = GLA_HEADS // hb
    tokw = lambda w, off, j: pl.BlockSpec((1, c, w), lambda bi, hi, ci, off=off, j=j: (bi, ci, off + hi * hb + j))
    heads = lambda w, off: [tokw(w, off, j) for j in range(hb)]
    aup = jnp.zeros((LANES, GLA_KEY), F32).at[:GLA_LORA].set(a_up)
    st = pl.BlockSpec((1, hb, dk, dv), lambda bi, hi, ci: (bi, hi, 0, 0))
    h, s1 = pl.pallas_call(
        functools.partial(_gla_kernel, c=c, hb=hb),
        grid=(b, nblk, t // c),
        in_specs=heads(dk, 0) + heads(dk, GLA_HEADS) + heads(dv, (2 * GLA_KEY) // dv)
        + heads(dv, (2 * GLA_KEY + D_INNER) // dv)
        + [pl.BlockSpec((1, c, LANES), lambda bi, hi, ci: (bi, ci, (2 * GLA_KEY + 2 * D_INNER) // LANES)),
           pl.BlockSpec((LANES, hb * dk), lambda bi, hi, ci: (0, hi)),
           pl.BlockSpec((1, hb * dk), lambda bi, hi, ci: (0, hi)),
           pl.BlockSpec((1, dv), lambda bi, hi, ci: (0, 0)), st],
        out_specs=[pl.BlockSpec((1, c, hb * dv), lambda bi, hi, ci: (bi, ci, hi)), st],
        out_shape=[jax.ShapeDtypeStruct((b, t, D_INNER), BF16), jax.ShapeDtypeStruct(s0.shape, F32)],
        scratch_shapes=[pltpu.VMEM((hb, dk, dv), F32)],
        compiler_params=_params(("parallel", "parallel", "arbitrary")),
        name="gla",
    )(*([p] * (4 * hb + 1)), aup, a_bias.reshape(1, -1), norm_w.reshape(1, -1), s0)
    return h.reshape(b * t, D_INNER), s1


def _trunk(x, pos, states, norm_gains, final_norm, rwkv_p, ret_p, ssd_p, gla_p):
    shift, wkv, ret_s, conv_s, ssd_s, gla_s = states
    b, t, d = x.shape
    x2 = x.reshape(b * t, d)

    def normed(i):
        return _rmsnorm(x2, norm_gains[i], BF16).reshape(b, t, d)

    (w_in, mu, w0, w_up, a0, a_up, k_k, k_a, r_k, ln_w, ln_b, w_out) = rwkv_p
    shift_new = _rmsnorm(x[:, -1], norm_gains[0], F32)
    h, wkv = _rwkv_mixer(normed(0), shift, wkv, w_in, mu, w0, w_up, a0, a_up, k_k, k_a, r_k, ln_w, ln_b)
    w_out_nh = w_out.reshape(RWKV_HEADS, RWKV_HEAD, d).swapaxes(0, 1).reshape(D_INNER, d)
    x2 = _matmul(h, w_out_nh, res=x2, tn=512)
    w_in, w_out = ret_p
    h, ret_s = _ret_mixer(normed(1), pos, ret_s, w_in)
    x2 = _matmul(h, w_out, res=x2, tn=512)
    (w_in, conv_w, conv_b, dt_bias, a_log, d_skip, norm_w, w_out) = ssd_p
    h, conv_s, ssd_s = _ssd_mixer(normed(2), conv_s, ssd_s, w_in, conv_w, conv_b, dt_bias, a_log, d_skip, norm_w)
    x2 = _matmul(h, w_out, res=x2, tn=512)
    (w_in, a_up, a_bias, norm_w, w_out) = gla_p
    h, gla_s = _gla_mixer(normed(3), gla_s, w_in, a_up, a_bias, norm_w)
    x2 = _matmul(h, w_out, res=x2, tn=512)

    y = _rmsnorm(x2, final_norm, F32).reshape(b, t, d)
    return y, (shift_new, wkv, ret_s, conv_s, ssd_s, gla_s)


def kernel(x_prompt, x_sample, state_rwkv_shift, state_rwkv_wkv, state_ret, state_ssd_conv, state_ssd, state_gla, norm_gains, final_norm, rwkv_w_in, rwkv_mu, rwkv_w0, rwkv_w_up, rwkv_a0, rwkv_a_up, rwkv_k_k, rwkv_k_a, rwkv_r_k, rwkv_ln_w, rwkv_ln_b, rwkv_w_out, ret_w_in, ret_w_out, ssd_w_in, ssd_conv_w, ssd_conv_b, ssd_dt_bias, ssd_A_log, ssd_D, ssd_norm_w, ssd_w_out, gla_w_in, gla_a_up, gla_a_bias, gla_norm_w, gla_w_out):
    rwkv_p = (rwkv_w_in, rwkv_mu, rwkv_w0, rwkv_w_up, rwkv_a0, rwkv_a_up, rwkv_k_k, rwkv_k_a, rwkv_r_k,
              rwkv_ln_w, rwkv_ln_b, rwkv_w_out)
    ret_p = (ret_w_in, ret_w_out)
    ssd_p = (ssd_w_in, ssd_conv_w, ssd_conv_b, ssd_dt_bias, ssd_A_log, ssd_D, ssd_norm_w, ssd_w_out)
    gla_p = (gla_w_in, gla_a_up, gla_a_bias, gla_norm_w, gla_w_out)

    bp, tp = x_prompt.shape[0], x_prompt.shape[1]
    zeros = lambda ref: jnp.zeros((bp,) + ref.shape[1:], F32)
    init_prompt = (None, zeros(state_rwkv_wkv), zeros(state_ret), zeros(state_ssd_conv),
                   zeros(state_ssd), zeros(state_gla))
    y_p, st_p = _trunk(x_prompt, jnp.arange(tp), init_prompt, norm_gains, final_norm, rwkv_p, ret_p, ssd_p, gla_p)
    init_sample = (state_rwkv_shift, state_rwkv_wkv, state_ret, state_ssd_conv, state_ssd, state_gla)
    y_s, st_s = _trunk(x_sample, PAST_LEN + jnp.arange(x_sample.shape[1]), init_sample, norm_gains, final_norm,
                       rwkv_p, ret_p, ssd_p, gla_p)
    return (y_p, y_s) + tuple(st_p) + tuple(st_s)
```

```python
import functools

import jax
import jax.numpy as jnp
from jax import lax
from jax.experimental import pallas as pl
from jax.experimental.pallas import tpu as pltpu

F32 = jnp.float32
BF16 = jnp.bfloat16

D_MODEL = 2048
D_INNER = 2 * D_MODEL
NORM_EPS = 1e-6
PAST_LEN = 16384

RWKV_HEAD = 64
RWKV_HEADS = D_INNER // RWKV_HEAD
RWKV_LORA = 96
RWKV_GN_EPS = 64e-5

RET_HEADS = 8
RET_DK = D_MODEL // RET_HEADS
RET_DV = D_INNER // RET_HEADS
ROPE_BASE = 10000.0

SSD_HEADDIM = 64
SSD_HEADS = D_INNER // SSD_HEADDIM
SSD_STATE = 128
SSD_GROUPS = 8
SSD_RPG = SSD_HEADS // SSD_GROUPS
SSD_CONV = 4
SSD_NORM_EPS = 1e-5
SSD_GW = D_INNER // SSD_GROUPS

GLA_HEADS = 4
GLA_KEY = D_MODEL // 2
GLA_DK = GLA_KEY // GLA_HEADS
GLA_DV = D_INNER // GLA_HEADS
GLA_LORA = 16
GLA_LOGIT_NORM = 16.0
GLA_SUB = 16

LANES = 128
VMEM_LIMIT = 52 * 1024 * 1024


def _params(sem):
    return pltpu.CompilerParams(dimension_semantics=sem, vmem_limit_bytes=VMEM_LIMIT)


def _sigmoid(x):
    return 1.0 / (1.0 + jnp.exp(-x))


def _silu(x):
    return x * _sigmoid(x)


def _softplus(x):
    return jnp.maximum(x, 0.0) + jnp.log(1.0 + jnp.exp(-jnp.abs(x)))


def _bdot(a, b):
    return jnp.dot(a.astype(BF16), b.astype(BF16), preferred_element_type=F32)


def _bdot_nt(a, b):
    return lax.dot_general(a.astype(BF16), b.astype(BF16), (((1,), (1,)), ((), ())),
                           preferred_element_type=F32)


def _bdot_tn(a, b):
    return lax.dot_general(a.astype(BF16), b.astype(BF16), (((0,), (0,)), ((), ())),
                           preferred_element_type=F32)


def _cumsum_rows(x):
    c = x.shape[0]
    row = lax.broadcasted_iota(jnp.int32, x.shape, 0)
    if c % 8 == 0:
        s = 1
        while s < c:
            x = x + jnp.where(row >= s, pltpu.roll(x, s, axis=0), 0.0)
            s *= 2
        return x
    acc = jnp.zeros_like(x)
    for j in range(c):
        acc = acc + jnp.where(row >= j, x[j:j + 1, :], 0.0)
    return acc


def _norm_kernel(x_ref, g_ref, o_ref):
    x = x_ref[...]
    ms = jnp.mean(x * x, axis=-1, keepdims=True)
    o_ref[...] = (x * lax.rsqrt(ms + NORM_EPS) * g_ref[...]).astype(o_ref.dtype)


def _rmsnorm(x2d, gain, out_dtype):
    m, d = x2d.shape
    tm = 512 if m % 512 == 0 else m
    return pl.pallas_call(
        _norm_kernel,
        grid=(m // tm,),
        in_specs=[pl.BlockSpec((tm, d), lambda i: (i, 0)), pl.BlockSpec((1, d), lambda i: (0, 0))],
        out_specs=pl.BlockSpec((tm, d), lambda i: (i, 0)),
        out_shape=jax.ShapeDtypeStruct((m, d), out_dtype),
        compiler_params=_params(("parallel",)),
        name="rmsnorm",
    )(x2d, gain.reshape(1, d))


def _mm_kernel(a_ref, w_ref, *rest, has_res):
    if has_res:
        r_ref, o_ref, wb_ref = rest
    else:
        o_ref, wb_ref = rest

    @pl.when(pl.program_id(1) == 0)
    def _():
        wb_ref[...] = w_ref[...].astype(BF16)

    acc = jnp.dot(a_ref[...], wb_ref[...], preferred_element_type=F32)
    if has_res:
        acc = r_ref[...] + acc
    o_ref[...] = acc.astype(o_ref.dtype)


def _matmul(a, w, res=None, tn=1024):
    m, k = a.shape
    n = w.shape[1]
    tm = next((c for c in (1024, 512) if m % c == 0), m)
    tn = min(tn, n)
    grid = (pl.cdiv(n, tn), m // tm)
    in_specs = [pl.BlockSpec((tm, k), lambda j, i: (i, 0)), pl.BlockSpec((k, tn), lambda j, i: (0, j))]
    args = [a, w]
    if res is not None:
        in_specs.append(pl.BlockSpec((tm, tn), lambda j, i: (i, j)))
        args.append(res)
    return pl.pallas_call(
        functools.partial(_mm_kernel, has_res=res is not None),
        grid=grid,
        in_specs=in_specs,
        out_specs=pl.BlockSpec((tm, tn), lambda j, i: (i, j)),
        out_shape=jax.ShapeDtypeStruct((m, n), F32),
        scratch_shapes=[pltpu.VMEM((k, tn), BF16)],
        compiler_params=_params(("arbitrary", "arbitrary")),
        name="matmul",
    )(*args)


def _shifted(buf_ref, carry_ref, x, c):
    if c % 8 == 0:
        row = lax.broadcasted_iota(jnp.int32, x.shape, 0)
        prev = jnp.where(row == 0, carry_ref[...], pltpu.roll(x, 1, axis=0))
        carry_ref[...] = x[c - 1:c, :]
        return prev
    buf_ref[7:8, :] = carry_ref[...]
    buf_ref[8:8 + c, :] = x
    prev = buf_ref[7:7 + c, :]
    carry_ref[...] = buf_ref[8 + c - 1:8 + c, :]
    return prev


def _head_sum(x):
    s = x[:, :LANES]
    for m in range(1, x.shape[1] // LANES):
        s = s + x[:, m * LANES:(m + 1) * LANES]
    return s + pltpu.roll(s, RWKV_HEADS, axis=1)


def _rwkv_kernel(pr_ref, pk_ref, pv_ref, pg_ref, pl_ref, sr_ref, sk_ref, sv_ref, sg_ref, sl_ref,
                 mr_ref, mk_ref, mv_ref, mg_ref, ml_ref, pvec_ref, lnw_ref, lnb_ref, wup_ref, aup_ref, s0_ref,
                 h_ref, s_out_ref,
                 st_ref, buf_ref, bufl_ref, cr_ref, ck_ref, cv_ref, cg_ref, cl_ref,
                 rb, wb, kb, ab, bb, vb, vt_ref, yt_ref, *, c):
    n = RWKV_HEAD
    nk = n // 2

    @pl.when(pl.program_id(1) == 0)
    def _():
        st_ref[...] = s0_ref[0]
        cr_ref[...] = sr_ref[0]
        ck_ref[...] = sk_ref[0]
        cv_ref[...] = sv_ref[0]
        cg_ref[...] = sg_ref[0]
        cl_ref[...] = sl_ref[0]

    def lerp(p_ref, carry_ref, mu_ref, buf):
        p = p_ref[0]
        return p + mu_ref[...] * (_shifted(buf, carry_ref, p, c) - p)

    low_half = lax.broadcasted_iota(jnp.int32, (c, LANES), 1) < RWKV_HEADS

    def dup_rows(dst, x):
        for m in range(nk):
            slab = x[:, m * LANES:(m + 1) * LANES]
            swapped = pltpu.roll(slab, RWKV_HEADS, axis=1)
            dst[2 * m] = jnp.where(low_half, slab, swapped)
            dst[2 * m + 1] = jnp.where(low_half, swapped, slab)

    w0, a0, k_k, k_a, r_k = (pvec_ref[i:i + 1, :] for i in range(5))
    lo = lerp(pl_ref, cl_ref, ml_ref, bufl_ref)
    lane_lo = lax.broadcasted_iota(jnp.int32, lo.shape, 1)
    lo = jnp.where(lane_lo < 2 * RWKV_LORA, lo, 0.0)
    dup_rows(wb, jnp.exp(-jnp.exp(-_softplus(-(w0 + _bdot(jnp.tanh(lo), wup_ref[...]))) - 0.5)))
    a = _sigmoid(a0 + _bdot(lo, aup_ref[...]))
    k = lerp(pk_ref, ck_ref, mk_ref, buf_ref)
    kk32 = k * k_k
    nrm = jnp.sqrt(_head_sum(kk32 * kk32))
    kk = kk32 / jnp.tile(jnp.maximum(nrm, 1e-12), (1, nk))
    dup_rows(ab, -kk)
    dup_rows(bb, kk * a)
    k2 = k * (1.0 + (a - 1.0) * k_a)
    dup_rows(kb, k2)
    r = lerp(pr_ref, cr_ref, mr_ref, buf_ref)
    coef = _head_sum(r * k2 * r_k)
    dup_rows(rb, r)
    v = lerp(pv_ref, cv_ref, mv_ref, buf_ref)
    for m in range(nk):
        vb[m] = v[:, m * LANES:(m + 1) * LANES]
    vt_ref[...] = jnp.swapaxes(vb[...], 0, 1)

    def step(t, carry):
        def pass1(kq, sa):
            return sa + st_ref[kq] * ab[kq, pl.ds(t, 1), :]
        sa = lax.fori_loop(0, n, pass1, jnp.zeros((nk, LANES), F32), unroll=32)
        vt = vt_ref[t]

        def pass2(kq, y):
            s_new = (st_ref[kq] * wb[kq, pl.ds(t, 1), :] + sa * bb[kq, pl.ds(t, 1), :]
                     + vt * kb[kq, pl.ds(t, 1), :])
            st_ref[kq] = s_new
            return y + s_new * rb[kq, pl.ds(t, 1), :]
        yt_ref[t] = lax.fori_loop(0, n, pass2, jnp.zeros((nk, LANES), F32), unroll=32)
        return carry

    lax.fori_loop(0, c, step, 0)

    def over_values(x):
        s = jnp.sum(x, axis=0)
        return (s + pltpu.roll(s, RWKV_HEADS, axis=1)) * (1.0 / n)

    y3 = jnp.swapaxes(yt_ref[...], 0, 1)
    d = y3 - over_values(y3)[None]
    var = over_values(d * d)
    hv = d * lax.rsqrt(var + RWKV_GN_EPS)[None] * lnw_ref[...] + lnb_ref[...] + coef[None] * vb[...]
    sg = _silu(lerp(pg_ref, cg_ref, mg_ref, buf_ref))
    for m in range(nk):
        sl = slice(m * LANES, (m + 1) * LANES)
        h_ref[0, :, sl] = (hv[m] * sg[:, sl]).astype(h_ref.dtype)

    @pl.when(pl.program_id(1) == pl.num_programs(1) - 1)
    def _():
        s_out_ref[0] = st_ref[...]


def _rwkv_mixer(xn, shift, wkv, w_in, mu, w0, w_up, a0, a_up, k_k, k_a, r_k, ln_w, ln_b):
    b, t, d = xn.shape
    hh, n, e = RWKV_HEADS, RWKV_HEAD, D_INNER
    nk = n // 2
    lw = 2 * LANES
    lblk = (4 * e) // lw
    to_nh = lambda x: x.reshape(x.shape[:-1] + (hh, n)).swapaxes(-1, -2)
    flat = lambda x: x.reshape(x.shape[:-2] + (e,))
    w_perm = jnp.concatenate([flat(to_nh(w_in[:, :4 * e].reshape(d, 4, e))).reshape(d, 4 * e), w_in[:, 4 * e:]], axis=1)
    mu_perm = jnp.concatenate([flat(to_nh(mu[:4 * e].reshape(4, e))).reshape(4 * e), mu[4 * e:]]).reshape(1, -1)
    p = _matmul(xn.reshape(b * t, d), w_perm).reshape(b, t, -1)
    if shift is None:
        p0 = jnp.zeros((b, 1, p.shape[-1]), F32)
    else:
        p0 = _matmul(shift.astype(BF16), w_perm).reshape(b, 1, -1)
    wup = jnp.zeros((lw, e), F32).at[:RWKV_LORA].set(flat(to_nh(w_up))).astype(BF16)
    aup = jnp.zeros((lw, e), F32).at[RWKV_LORA:2 * RWKV_LORA].set(flat(to_nh(a_up))).astype(BF16)
    pvec = jnp.stack([flat(to_nh(x)) for x in (w0, a0, k_k, k_a)] + [r_k.T.reshape(e)])
    slabs = lambda x: flat(to_nh(x)).reshape(nk, 1, LANES)
    lnw3, lnb3 = slabs(ln_w), slabs(ln_b)
    s0 = wkv.reshape(b, hh, nk, 2, n).transpose(0, 4, 2, 3, 1).reshape(b, n, nk, LANES)

    c = min(t, 32)
    big = lambda j: pl.BlockSpec((1, c, e), lambda bi, ci, j=j: (bi, ci, j))
    big0 = lambda j: pl.BlockSpec((1, 1, e), lambda bi, ci, j=j: (bi, 0, j))
    mub = lambda j: pl.BlockSpec((1, e), lambda bi, ci, j=j: (0, j))
    const = lambda shp: pl.BlockSpec(shp, lambda bi, ci: (0,) * len(shp))
    st = pl.BlockSpec((1, n, nk, LANES), lambda bi, ci: (bi, 0, 0, 0))
    rows = pltpu.VMEM((n, c, LANES), F32)
    vals = pltpu.VMEM((nk, c, LANES), F32)
    tiles = pltpu.VMEM((c, nk, LANES), F32)
    h, s1 = pl.pallas_call(
        functools.partial(_rwkv_kernel, c=c),
        grid=(b, t // c),
        in_specs=([big(j) for j in range(4)] + [pl.BlockSpec((1, c, lw), lambda bi, ci: (bi, ci, lblk))]
                  + [big0(j) for j in range(4)] + [pl.BlockSpec((1, 1, lw), lambda bi, ci: (bi, 0, lblk))]
                  + [mub(j) for j in range(4)] + [pl.BlockSpec((1, lw), lambda bi, ci: (0, lblk))]
                  + [const((5, e)), const((nk, 1, LANES)), const((nk, 1, LANES)), const((lw, e)), const((lw, e)), st]),
        out_specs=[pl.BlockSpec((1, c, e), lambda bi, ci: (bi, ci, 0)), st],
        out_shape=[jax.ShapeDtypeStruct((b, t, e), BF16), jax.ShapeDtypeStruct((b, n, nk, LANES), F32)],
        scratch_shapes=[pltpu.VMEM((n, nk, LANES), F32), pltpu.VMEM((8 + c, e), F32), pltpu.VMEM((8 + c, lw), F32)]
        + [pltpu.VMEM((1, e), F32)] * 4 + [pltpu.VMEM((1, lw), F32)] + [rows] * 5 + [vals] + [tiles] * 2,
        compiler_params=_params(("parallel", "arbitrary")),
        name="rwkv",
    )(p, p, p, p, p, p0, p0, p0, p0, p0, mu_perm, mu_perm, mu_perm, mu_perm, mu_perm,
      pvec, lnw3, lnb3, wup, aup, s0)
    wkv_new = s1.reshape(b, n, nk, 2, hh).transpose(0, 4, 2, 3, 1).reshape(b, hh, n, n)
    return h.reshape(b * t, e), wkv_new


def _ret_kernel(lg_ref, q_ref, k_ref, v_ref, g_ref, cos_ref, sin_ref, s0_ref, h_ref, s_out_ref, s_ref, *, c, hb):
    @pl.when(pl.program_id(2) == 0)
    def _():
        s_ref[...] = s0_ref[0]

    cos = cos_ref[...]
    sin = sin_ref[...]
    half = RET_DK // 2

    def rot(x):
        x1, x2 = x[:, :half], x[:, half:]
        return jnp.concatenate([x1 * cos - x2 * sin, x1 * sin + x2 * cos], axis=1)

    ti = lax.broadcasted_iota(jnp.int32, (c, c), 0)
    si = lax.broadcasted_iota(jnp.int32, (c, c), 1)
    causal = ti >= si
    dist = (ti - si).astype(F32)
    trow = lax.broadcasted_iota(jnp.int32, (c, RET_DK), 0).astype(F32)
    for j in range(hb):
        lg = lg_ref[pl.program_id(1) * hb + j]
        q = rot(q_ref[0, :, j * RET_DK:(j + 1) * RET_DK])
        k = rot(k_ref[0, :, j * RET_DK:(j + 1) * RET_DK]) * (RET_DK ** -0.5)
        v = v_ref[0, :, j * RET_DV:(j + 1) * RET_DV]
        dmat = jnp.exp(jnp.where(causal, dist * lg, -jnp.inf))
        s = s_ref[j]
        att = _bdot_nt(q, k) * dmat
        y = _bdot(att, v) + _bdot(q * jnp.exp((trow + 1.0) * lg), s)
        s_ref[j] = (jnp.exp(jnp.zeros((1, RET_DV), F32) + lg * c) * s
                    + _bdot_tn(k * jnp.exp((c - 1.0 - trow) * lg), v))
        yn = y * lax.rsqrt(jnp.mean(y * y, axis=-1, keepdims=True) + NORM_EPS)
        h_ref[0, :, j * RET_DV:(j + 1) * RET_DV] = (
            yn * _silu(g_ref[0, :, j * RET_DV:(j + 1) * RET_DV])).astype(h_ref.dtype)

    @pl.when(pl.program_id(2) == pl.num_programs(2) - 1)
    def _():
        s_out_ref[0] = s_ref[...]


def _ret_mixer(xn, pos, s0, w_in):
    b, t, d = xn.shape
    p = _matmul(xn.reshape(b * t, d), w_in).reshape(b, t, -1)
    c = min(t, 256)
    hb = 4 if t > 8 else RET_HEADS
    half = RET_DK // 2
    inv = 1.0 / (ROPE_BASE ** jnp.linspace(0.0, 1.0, half, dtype=F32))
    ang = pos.astype(F32)[:, None] * inv[None, :]
    cos, sin = jnp.cos(ang), jnp.sin(ang)
    log_gamma = jnp.log1p(-(2.0 ** (-5.0 - jnp.arange(RET_HEADS, dtype=F32))))
    nblk = RET_HEADS // hb
    spec = lambda w, off: pl.BlockSpec((1, c, hb * w), lambda bi, hi, ci, lg, off=off: (bi, ci, off + hi))
    tab = pl.BlockSpec((c, half), lambda bi, hi, ci, lg: (ci, 0))
    st = pl.BlockSpec((1, hb, RET_DK, RET_DV), lambda bi, hi, ci, lg: (bi, hi, 0, 0))
    h, s1 = pl.pallas_call(
        functools.partial(_ret_kernel, c=c, hb=hb),
        grid_spec=pltpu.PrefetchScalarGridSpec(
            num_scalar_prefetch=1,
            grid=(b, nblk, t // c),
            in_specs=[spec(RET_DK, 0), spec(RET_DK, nblk), spec(RET_DV, nblk), spec(RET_DV, 2 * nblk), tab, tab, st],
            out_specs=[pl.BlockSpec((1, c, hb * RET_DV), lambda bi, hi, ci, lg: (bi, ci, hi)), st],
            scratch_shapes=[pltpu.VMEM((hb, RET_DK, RET_DV), F32)],
        ),
        out_shape=[jax.ShapeDtypeStruct((b, t, D_INNER), BF16), jax.ShapeDtypeStruct(s0.shape, F32)],
        compiler_params=_params(("parallel", "parallel", "arbitrary")),
        name="retention",
    )(log_gamma, p, p, p, p, cos, sin, s0)
    return h.reshape(b * t, D_INNER), s1


def _causal_conv(buf_ref, x, w, bias, c):
    buf_ref[8:8 + c, :] = x
    acc = bias
    for j in range(SSD_CONV):
        acc = acc + w[j:j + 1, :] * buf_ref[5 + j:5 + j + c, :]
    buf_ref[5:8, :] = buf_ref[5 + c:8 + c, :]
    return acc


def _ssd_kernel(z_ref, x_ref, bm_ref, cm_ref, dt_ref, cx_ref, cb_ref, cc_ref,
                wx_ref, wb_ref, wc_ref, bx_ref, bb_ref, bc_ref, dtb_ref, alog_ref, dsk_ref, nw_ref, s0_ref,
                h_ref, s_out_ref, bufx_ref, bufb_ref, bufc_ref, s_ref, *, c, gb):
    @pl.when(pl.program_id(2) == 0)
    def _():
        s_ref[...] = s0_ref[0]
        bufx_ref[5:8, :] = cx_ref[0]
        bufb_ref[5:8, :] = cb_ref[0]
        bufc_ref[5:8, :] = cc_ref[0]

    xs_all = _silu(_causal_conv(bufx_ref, x_ref[0], wx_ref[...], bx_ref[...], c))
    bm_all = _silu(_causal_conv(bufb_ref, bm_ref[0], wb_ref[...], bb_ref[...], c))
    cm_all = _silu(_causal_conv(bufc_ref, cm_ref[0], wc_ref[...], bc_ref[...], c))

    lane = lax.broadcasted_iota(jnp.int32, (c, LANES), 1)
    dt = jnp.where(lane < SSD_HEADS, _softplus(dt_ref[0] + dtb_ref[...]), 0.0)
    la = dt * (-jnp.exp(alog_ref[...]))
    lcum = _cumsum_rows(la)

    ti = lax.broadcasted_iota(jnp.int32, (c, c), 0)
    si = lax.broadcasted_iota(jnp.int32, (c, c), 1)
    causal = ti >= si
    eye = ti == si
    last = lax.broadcasted_iota(jnp.int32, (c, 1), 0) == c - 1
    gw, ns = SSD_GW, SSD_STATE
    for gl in range(gb):
        xs = xs_all[:, gl * gw:(gl + 1) * gw]
        bm = bm_all[:, gl * ns:(gl + 1) * ns]
        cm = cm_all[:, gl * ns:(gl + 1) * ns]
        scores = _bdot_nt(cm, bm)
        ys = []
        for r in range(SSD_RPG):
            head = (pl.program_id(1) * gb + gl) * SSD_RPG + r
            sel = lane == head
            lcol = jnp.sum(jnp.where(sel, lcum, 0.0), axis=-1, keepdims=True)
            dtc = jnp.sum(jnp.where(sel, dt, 0.0), axis=-1, keepdims=True)
            lrow = jnp.sum(jnp.where(eye, lcol, 0.0), axis=0, keepdims=True)
            lend = jnp.sum(jnp.where(last, lcol, 0.0), axis=0, keepdims=True)
            dec = jnp.exp(jnp.where(causal, lcol - lrow, -jnp.inf))
            v_r = xs[:, r * SSD_HEADDIM:(r + 1) * SSD_HEADDIM] * dtc
            s_r = s_ref[gl * SSD_RPG + r]
            ys.append(_bdot(scores * dec, v_r) + jnp.exp(lcol) * _bdot_nt(cm, s_r))
            s_ref[gl * SSD_RPG + r] = jnp.exp(lend) * s_r + _bdot_tn(jnp.exp(lend - lcol) * v_r, bm)
        sl = slice(gl * gw, (gl + 1) * gw)
        y = jnp.concatenate(ys, axis=1) + xs * dsk_ref[:, sl]
        y = y * _silu(z_ref[0, :, sl])
        yn = y * lax.rsqrt(jnp.mean(y * y, axis=-1, keepdims=True) + SSD_NORM_EPS)
        h_ref[0, :, sl] = (yn * nw_ref[:, sl]).astype(h_ref.dtype)

    @pl.when(pl.program_id(2) == pl.num_programs(2) - 1)
    def _():
        s_out_ref[0] = s_ref[...]


def _ssd_mixer(xn, conv_state, s0, w_in, conv_w, conv_b, dt_bias, a_log, d_skip, norm_w):
    b, t, d = xn.shape
    p = _matmul(xn.reshape(b * t, d), w_in).reshape(b, t, -1)
    c = min(t, 128)
    gb = 4 if t > 8 else SSD_GROUPS
    e, ng = D_INNER, SSD_GROUPS
    gw, st = gb * SSD_GW, gb * SSD_STATE
    nblk = ng // gb
    xoff = e // gw
    boff = (2 * e) // st
    coff = boff + nblk
    dtoff = (2 * e + 2 * ng * SSD_STATE) // LANES
    tokw = lambda w, off: pl.BlockSpec((1, c, w), lambda bi, gi, ci, off=off: (bi, ci, off + gi))
    cs = lambda w, off: pl.BlockSpec((1, SSD_CONV - 1, w), lambda bi, gi, ci, off=off: (bi, 0, off + gi))
    cw = lambda w, off: pl.BlockSpec((SSD_CONV, w), lambda bi, gi, ci, off=off: (0, off + gi))
    cbv = lambda w, off: pl.BlockSpec((1, w), lambda bi, gi, ci, off=off: (0, off + gi))
    hv = pl.BlockSpec((1, LANES), lambda bi, gi, ci: (0, 0))
    pad_heads = lambda x: jnp.zeros((1, LANES), F32).at[0, :SSD_HEADS].set(x)
    cb2 = conv_b.reshape(1, -1)
    nh = gb * SSD_RPG
    s0t = jnp.swapaxes(s0, 2, 3)
    stspec = pl.BlockSpec((1, nh, SSD_HEADDIM, SSD_STATE), lambda bi, gi, ci: (bi, gi, 0, 0))
    h, s1 = pl.pallas_call(
        functools.partial(_ssd_kernel, c=c, gb=gb),
        grid=(b, nblk, t // c),
        in_specs=[tokw(gw, 0), tokw(gw, xoff), tokw(st, boff), tokw(st, coff),
                  pl.BlockSpec((1, c, LANES), lambda bi, gi, ci: (bi, ci, dtoff)),
                  cs(gw, 0), cs(st, e // st), cs(st, e // st + nblk),
                  cw(gw, 0), cw(st, e // st), cw(st, e // st + nblk),
                  cbv(gw, 0), cbv(st, e // st), cbv(st, e // st + nblk),
                  hv, hv, cbv(gw, 0), cbv(gw, 0), stspec],
        out_specs=[tokw(gw, 0), stspec],
        out_shape=[jax.ShapeDtypeStruct((b, t, e), BF16), jax.ShapeDtypeStruct(s0t.shape, F32)],
        scratch_shapes=[pltpu.VMEM((8 + c, gw), F32), pltpu.VMEM((8 + c, st), F32), pltpu.VMEM((8 + c, st), F32),
                        pltpu.VMEM((nh, SSD_HEADDIM, SSD_STATE), F32)],
        compiler_params=_params(("parallel", "parallel", "arbitrary")),
        name="ssd",
    )(p, p, p, p, p, conv_state, conv_state, conv_state, conv_w, conv_w, conv_w, cb2, cb2, cb2,
      pad_heads(dt_bias), pad_heads(a_log), jnp.repeat(d_skip, SSD_HEADDIM).reshape(1, e),
      norm_w.reshape(1, e), s0t)
    conv_new = p[:, t - (SSD_CONV - 1):, e:e + e + 2 * ng * SSD_STATE]
    return h.reshape(b * t, e), conv_new, jnp.swapaxes(s1, 2, 3)


def _gla_kernel(*refs, c, hb):
    q_refs, k_refs, v_refs, g_refs = (refs[i * hb:(i + 1) * hb] for i in range(4))
    ad_ref, aup_ref, ab_ref, nw_ref, s0_ref, h_ref, s_out_ref, s_ref = refs[4 * hb:]

    @pl.when(pl.program_id(2) == 0)
    def _():
        s_ref[...] = s0_ref[0]

    lane = lax.broadcasted_iota(jnp.int32, (c, LANES), 1)
    ad = jnp.where(lane < GLA_LORA, ad_ref[0], 0.0)
    la_all = -_softplus(-(_bdot(ad, aup_ref[...]) + ab_ref[...])) / GLA_LOGIT_NORM
    sub = min(c, GLA_SUB)
    rowi = lax.broadcasted_iota(jnp.int32, (c, GLA_DK), 0)
    eye = (lax.broadcasted_iota(jnp.int32, (GLA_DK, GLA_DK), 0)
           == lax.broadcasted_iota(jnp.int32, (GLA_DK, GLA_DK), 1))
    for j in range(hb):
        lcum = _cumsum_rows(la_all[:, j * GLA_DK:(j + 1) * GLA_DK])
        q = q_refs[j][0] * (GLA_DK ** -0.5)
        k = k_refs[j][0]
        v = v_refs[j][0]
        lend = jnp.sum(jnp.where(rowi == c - 1, lcum, 0.0), axis=0, keepdims=True)
        kend = k * jnp.exp(lend - lcum)
        s = s_ref[j]
        ys = []
        for i in range(c // sub):
            lo, hi = i * sub, (i + 1) * sub
            base = lcum[lo - 1:lo] if i else jnp.zeros((1, GLA_DK), F32)
            lloc = lcum[lo:hi] - base
            qe = q[lo:hi] * jnp.exp(lloc)
            keys = [k[m * sub:(m + 1) * sub] * jnp.exp(base - lcum[m * sub:(m + 1) * sub]) for m in range(i)]
            keys.append(k[lo:hi] * jnp.exp(-lloc))
            att = _bdot_nt(qe, jnp.concatenate(keys, axis=0) if i else keys[0])
            ti = lax.broadcasted_iota(jnp.int32, (sub, hi), 0) + lo
            si = lax.broadcasted_iota(jnp.int32, (sub, hi), 1)
            ys.append(_bdot(jnp.where(ti >= si, att, 0.0), v[:hi]))
        y = (jnp.concatenate(ys, axis=0) if len(ys) > 1 else ys[0]) + _bdot(q * jnp.exp(lcum), s)
        gend_col = jnp.sum(jnp.where(eye, jnp.exp(lend), 0.0), axis=1, keepdims=True)
        s_ref[j] = gend_col * s + _bdot_tn(kend, v)
        yn = y * lax.rsqrt(jnp.mean(y * y, axis=-1, keepdims=True) + NORM_EPS) * nw_ref[...]
        h_ref[0, :, j * GLA_DV:(j + 1) * GLA_DV] = (yn * _silu(g_refs[j][0])).astype(h_ref.dtype)

    @pl.when(pl.program_id(2) == pl.num_programs(2) - 1)
    def _():
        s_out_ref[0] = s_ref[...]


def _gla_mixer(xn, s0, w_in, a_up, a_bias, norm_w):
    b, t, d = xn.shape
    p = _matmul(xn.reshape(b * t, d), w_in).reshape(b, t, -1)
    c = min(t, 8 * GLA_SUB)
    hb = GLA_HEADS
    dk, dv = GLA_DK, GLA_DV
    nblk = GLA_HEADS // hb
    tokw = lambda w, off, j: pl.BlockSpec((1, c, w), lambda bi, hi, ci, off=off, j=j: (bi, ci, off + hi * hb + j))
    heads = lambda w, off: [tokw(w, off, j) for j in range(hb)]
    aup = jnp.zeros((LANES, GLA_KEY), F32).at[:GLA_LORA].set(a_up)
    st = pl.BlockSpec((1, hb, dk, dv), lambda bi, hi, ci: (bi, hi, 0, 0))
    h, s1 = pl.pallas_call(
        functools.partial(_gla_kernel, c=c, hb=hb),
        grid=(b, nblk, t // c),
        in_specs=heads(dk, 0) + heads(dk, GLA_HEADS) + heads(dv, (2 * GLA_KEY) // dv)
        + heads(dv, (2 * GLA_KEY + D_INNER) // dv)
        + [pl.BlockSpec((1, c, LANES), lambda bi, hi, ci: (bi, ci, (2 * GLA_KEY + 2 * D_INNER) // LANES)),
           pl.BlockSpec((LANES, hb * dk), lambda bi, hi, ci: (0, hi)),
           pl.BlockSpec((1, hb * dk), lambda bi, hi, ci: (0, hi)),
           pl.BlockSpec((1, dv), lambda bi, hi, ci: (0, 0)), st],
        out_specs=[pl.BlockSpec((1, c, hb * dv), lambda bi, hi, ci: (bi, ci, hi)), st],
        out_shape=[jax.ShapeDtypeStruct((b, t, D_INNER), BF16), jax.ShapeDtypeStruct(s0.shape, F32)],
        scratch_shapes=[pltpu.VMEM((hb, dk, dv), F32)],
        compiler_params=_params(("parallel", "parallel", "arbitrary")),
        name="gla",
    )(*([p] * (4 * hb + 1)), aup, a_bias.reshape(1, -1), norm_w.reshape(1, -1), s0)
    return h.reshape(b * t, D_INNER), s1


def _trunk(x, pos, states, norm_gains, final_norm, rwkv_p, ret_p, ssd_p, gla_p):
    shift, wkv, ret_s, conv_s, ssd_s, gla_s = states
    b, t, d = x.shape
    x2 = x.reshape(b * t, d)

    def normed(i):
        return _rmsnorm(x2, norm_gains[i], BF16).reshape(b, t, d)

    (w_in, mu, w0, w_up, a0, a_up, k_k, k_a, r_k, ln_w, ln_b, w_out) = rwkv_p
    shift_new = _rmsnorm(x[:, -1], norm_gains[0], F32)
    h, wkv = _rwkv_mixer(normed(0), shift, wkv, w_in, mu, w0, w_up, a0, a_up, k_k, k_a, r_k, ln_w, ln_b)
    w_out_nh = w_out.reshape(RWKV_HEADS, RWKV_HEAD, d).swapaxes(0, 1).reshape(D_INNER, d)
    x2 = _matmul(h, w_out_nh, res=x2, tn=512)
    w_in, w_out = ret_p
    h, ret_s = _ret_mixer(normed(1), pos, ret_s, w_in)
    x2 = _matmul(h, w_out, res=x2, tn=512)
    (w_in, conv_w, conv_b, dt_bias, a_log, d_skip, norm_w, w_out) = ssd_p
    h, conv_s, ssd_s = _ssd_mixer(normed(2), conv_s, ssd_s, w_in, conv_w, conv_b, dt_bias, a_log, d_skip, norm_w)
    x2 = _matmul(h, w_out, res=x2, tn=512)
    (w_in, a_up, a_bias, norm_w, w_out) = gla_p
    h, gla_s = _gla_mixer(normed(3), gla_s, w_in, a_up, a_bias, norm_w)
    x2 = _matmul(h, w_out, res=x2, tn=512)

    y = _rmsnorm(x2, final_norm, F32).reshape(b, t, d)
    return y, (shift_new, wkv, ret_s, conv_s, ssd_s, gla_s)


def kernel(x_prompt, x_sample, state_rwkv_shift, state_rwkv_wkv, state_ret, state_ssd_conv, state_ssd, state_gla, norm_gains, final_norm, rwkv_w_in, rwkv_mu, rwkv_w0, rwkv_w_up, rwkv_a0, rwkv_a_up, rwkv_k_k, rwkv_k_a, rwkv_r_k, rwkv_ln_w, rwkv_ln_b, rwkv_w_out, ret_w_in, ret_w_out, ssd_w_in, ssd_conv_w, ssd_conv_b, ssd_dt_bias, ssd_A_log, ssd_D, ssd_norm_w, ssd_w_out, gla_w_in, gla_a_up, gla_a_bias, gla_norm_w, gla_w_out):
    rwkv_p = (rwkv_w_in, rwkv_mu, rwkv_w0, rwkv_w_up, rwkv_a0, rwkv_a_up, rwkv_k_k, rwkv_k_a, rwkv_r_k,
              rwkv_ln_w, rwkv_ln_b, rwkv_w_out)
    ret_p = (ret_w_in, ret_w_out)
    ssd_p = (ssd_w_in, ssd_conv_w, ssd_conv_b, ssd_dt_bias, ssd_A_log, ssd_D, ssd_norm_w, ssd_w_out)
    gla_p = (gla_w_in, gla_a_up, gla_a_bias, gla_norm_w, gla_w_out)

    bp, tp = x_prompt.shape[0], x_prompt.shape[1]
    zeros = lambda ref: jnp.zeros((bp,) + ref.shape[1:], F32)
    init_prompt = (None, zeros(state_rwkv_wkv), zeros(state_ret), zeros(state_ssd_conv),
                   zeros(state_ssd), zeros(state_gla))
    y_p, st_p = _trunk(x_prompt, jnp.arange(tp), init_prompt, norm_gains, final_norm, rwkv_p, ret_p, ssd_p, gla_p)
    init_sample = (state_rwkv_shift, state_rwkv_wkv, state_ret, state_ssd_conv, state_ssd, state_gla)
    y_s, st_s = _trunk(x_sample, PAST_LEN + jnp.arange(x_sample.shape[1]), init_sample, norm_gains, final_norm,
                       rwkv_p, ret_p, ssd_p, gla_p)
    return (y_p, y_s) + tuple(st_p) + tuple(st_s)
```
